```python
import jax
import jax.numpy as jnp
from jax import lax
import numpy as np

D_MODEL = 1024
BATCH = 4
SEQ = 8192
DEPTH = 2

NORM_EPS = 1e-6
N_BRANCH = 3
N_ADA = 6

LRU_WIDTH = D_MODEL
LRU_BLOCKS = 16
LRU_BLOCK_DIM = LRU_WIDTH // LRU_BLOCKS
CONV_WIDTH = 4
LRU_C = 8.0

SB_HEADS = 8
SB_HEAD_DIM = D_MODEL // SB_HEADS
SB_WIDTH = SB_HEADS * SB_HEAD_DIM
Q_BLOCK = 128

RW_HEAD_DIM = 64
RW_HEADS = D_MODEL // RW_HEAD_DIM
RW_WIDTH = RW_HEADS * RW_HEAD_DIM
DECAY_LORA = 64
AAA_LORA = 64
GATE_LORA = 128
RW_IN = 3 * RW_WIDTH + DECAY_LORA + AAA_LORA + GATE_LORA
RW_GN_EPS = 1e-5 * RW_HEAD_DIM

N_IN = 2 * LRU_WIDTH + 3 * SB_WIDTH + RW_IN + N_BRANCH * D_MODEL

N_EXPERTS = 32
TOP_K = 4
D_FF = D_MODEL
SWIGLU_LIMIT = 7.0
SWIGLU_ALPHA = 1.702
EXPERT_BLOCK = 256

kernel_name = 'hybrid_rglru_stickbreak_rwkv7_moe_adaln'


def _split_at(t, widths):
    idx = [int(i) for i in np.cumsum(widths)[:-1]]
    return jnp.split(t, idx, axis=-1)


def rmsnorm(x, gain):
    xf = x.astype(jnp.float32)
    y = xf * lax.rsqrt(jnp.mean(xf * xf, axis=-1, keepdims=True) + NORM_EPS)
    return (y * gain.astype(jnp.float32)).astype(x.dtype)


def modulate(h, shift, scale):
    return h * (1 + scale[:, None, :]) + shift[:, None, :]


def shift_right(t, n):
    return jnp.pad(t, ((0, 0), (n, 0), (0, 0)))[:, : t.shape[1], :]


def _linear_combine(left, right):
    a_l, b_l = left
    a_r, b_r = right
    return a_l * a_r, a_r * b_l + b_r


def rglru_branch(x_in, x_gate, conv_w, conv_b, wa, ba, wx, bx, lam):
    bsz, seq, _ = x_in.shape
    xc = conv_b + sum(conv_w[k] * shift_right(x_in, k) for k in range(CONV_WIDTH))
    xb = xc.reshape(bsz, seq, LRU_BLOCKS, LRU_BLOCK_DIM)
    r = jax.nn.sigmoid(jnp.einsum('bsni,nij->bsnj', xb, wa).reshape(bsz, seq, LRU_WIDTH) + ba)
    i = jax.nn.sigmoid(jnp.einsum('bsni,nij->bsnj', xb, wx).reshape(bsz, seq, LRU_WIDTH) + bx)
    log_a = -LRU_C * r.astype(jnp.float32) * jax.nn.softplus(-lam.astype(jnp.float32))
    a = jnp.exp(log_a)
    b = jnp.sqrt(-jnp.expm1(2.0 * log_a)) * (i * xc).astype(jnp.float32)
    _, h = lax.associative_scan(_linear_combine, (a, b), axis=1)
    return jax.nn.gelu(x_gate) * h.astype(x_in.dtype)


def stick_breaking_attention(q, k, v):
    bsz, seq, n_heads, head_dim = q.shape
    n_blk = seq // Q_BLOCK
    f32 = jnp.float32
    qf = q.astype(f32) * (head_dim ** -0.5)
    kf = k.astype(f32)
    vf = v.astype(f32)
    qb = qf.reshape(bsz, n_blk, Q_BLOCK, n_heads, head_dim).transpose(1, 0, 3, 2, 4)
    kpos = jnp.arange(seq)

    def one_block(args):
        q_blk, blk = args
        qpos = blk * Q_BLOCK + jnp.arange(Q_BLOCK)
        z = jnp.einsum('bhqd,bkhd->bhqk', q_blk, kf)
        mask = kpos[None, :] < qpos[:, None]
        log_stay = jnp.where(mask, jax.nn.log_sigmoid(-z), 0.0)
        later = lax.cumsum(log_stay, axis=3, reverse=True) - log_stay
        attn = jnp.where(mask, jnp.exp(jax.nn.log_sigmoid(z) + later), 0.0)
        return jnp.einsum('bhqk,bkhd->bqhd', attn, vf)

    out = lax.map(one_block, (qb, jnp.arange(n_blk)))
    return out.transpose(1, 0, 2, 3, 4).reshape(bsz, seq, n_heads * head_dim).astype(q.dtype)


def rwkv7_branch(cols, mu, w0, w_up, a0, a_up, g_up, k_k, k_a, r_k, lnx_w, lnx_b):
    bsz, seq, _ = cols.shape
    f32 = jnp.float32
    cols = cols + mu * (shift_right(cols, 1) - cols)
    r, k, v, xw, xa, xg = _split_at(cols, (RW_WIDTH, RW_WIDTH, RW_WIDTH, DECAY_LORA, AAA_LORA, GATE_LORA))
    w_log = -jax.nn.softplus(-(w0 + jnp.tanh(xw) @ w_up).astype(f32)) - 0.5
    decay = jnp.exp(-jnp.exp(w_log))
    a = jax.nn.sigmoid(a0 + xa @ a_up)
    g = jax.nn.sigmoid(xg) @ g_up

    def heads(t):
        return t.reshape(bsz, seq, RW_HEADS, RW_HEAD_DIM).astype(f32)

    kk = heads(k * k_k)
    kk = kk / jnp.maximum(jnp.linalg.norm(kk, axis=-1, keepdims=True), 1e-12)
    k = k * (1 + (a - 1) * k_a)
    r_h, k_h, v_h, a_h, w_h = heads(r), heads(k), heads(v), heads(a), heads(decay)

    def step(state, inp):
        r_t, w_t, k_t, v_t, kk_t, a_t = inp
        sa = jnp.einsum('bhij,bhj->bhi', state, -kk_t)
        state = (state * w_t[:, :, None, :]
                 + sa[..., None] * (kk_t * a_t)[:, :, None, :]
                 + v_t[..., None] * k_t[:, :, None, :])
        return state, jnp.einsum('bhij,bhj->bhi', state, r_t)

    def time_major(t):
        return jnp.swapaxes(t, 0, 1)

    state0 = jnp.zeros((bsz, RW_HEADS, RW_HEAD_DIM, RW_HEAD_DIM), f32)
    _, y = lax.scan(step, state0, tuple(time_major(t) for t in (r_h, w_h, k_h, v_h, kk, a_h)))
    y = time_major(y)
    mean = jnp.mean(y, axis=-1, keepdims=True)
    var = jnp.mean(jnp.square(y - mean), axis=-1, keepdims=True)
    y = ((y - mean) * lax.rsqrt(var + RW_GN_EPS)).reshape(bsz, seq, RW_WIDTH) * lnx_w + lnx_b
    bonus = jnp.sum(r_h * k_h * r_k, axis=-1, keepdims=True) * v_h
    y = y + bonus.reshape(bsz, seq, RW_WIDTH)
    return (y * g).astype(cols.dtype)


def hybrid_mixer(h, w_in, conv_w, conv_b, lru_wa, lru_ba, lru_wx, lru_bx, lru_lambda,
                 rw_mu, rw_w0, rw_w_up, rw_a0, rw_a_up, rw_g_up, rw_k_k, rw_k_a, rw_r_k,
                 rw_lnx_w, rw_lnx_b, p_lru, p_sb, p_rwkv, w_out):
    bsz, seq, _ = h.shape
    cols = h @ w_in
    lru_x, lru_gate, q, k, v, rw_cols, gate_logits = _split_at(
        cols, (LRU_WIDTH, LRU_WIDTH, SB_WIDTH, SB_WIDTH, SB_WIDTH, RW_IN, N_BRANCH * D_MODEL))
    y_a = rglru_branch(lru_x, lru_gate, conv_w, conv_b, lru_wa, lru_ba, lru_wx, lru_bx, lru_lambda)

    def sb_heads(t):
        return t.reshape(bsz, seq, SB_HEADS, SB_HEAD_DIM)

    y_b = stick_breaking_attention(sb_heads(q), sb_heads(k), sb_heads(v))
    y_c = rwkv7_branch(rw_cols, rw_mu, rw_w0, rw_w_up, rw_a0, rw_a_up, rw_g_up,
                       rw_k_k, rw_k_a, rw_r_k, rw_lnx_w, rw_lnx_b)
    g_a, g_b, g_c = jnp.split(jax.nn.sigmoid(gate_logits), N_BRANCH, axis=-1)
    merged = g_a * (y_a @ p_lru) + g_b * (y_b @ p_sb) + g_c * (y_c @ p_rwkv)
    return merged @ w_out


def moe_ffn(h, w_router, b_router, w_gu, b_gu, w_down, b_down):
    bsz, seq, d = h.shape
    x = h.reshape(-1, d)
    n_tok = x.shape[0]
    logits = (x @ w_router + b_router).astype(jnp.float32)
    top_logit, top_idx = lax.top_k(logits, TOP_K)
    top_w = jax.nn.softmax(top_logit, axis=-1)
    n_assign = n_tok * TOP_K
    flat_e = top_idx.reshape(-1)
    order = jnp.argsort(flat_e)
    sorted_e = flat_e[order]
    counts = jnp.bincount(flat_e, length=N_EXPERTS)
    padded = (counts + EXPERT_BLOCK - 1) // EXPERT_BLOCK * EXPERT_BLOCK
    pad_end = jnp.cumsum(padded)
    pad_start = pad_end - padded
    start = jnp.cumsum(counts) - counts
    slot = pad_start[sorted_e] + jnp.arange(n_assign) - start[sorted_e]
    n_blocks = -(-n_assign // EXPERT_BLOCK) + N_EXPERTS
    n_slots = n_blocks * EXPERT_BLOCK
    slot_token = jnp.zeros((n_slots,), jnp.int32).at[slot].set((order // TOP_K).astype(jnp.int32))
    slot_weight = jnp.zeros((n_slots,), jnp.float32).at[slot].set(top_w.reshape(-1)[order])
    block_expert = jnp.minimum(
        jnp.searchsorted(pad_end, jnp.arange(n_blocks) * EXPERT_BLOCK, side='right'), N_EXPERTS - 1)

    def expert_block(args):
        tok, e = args
        xb = x[tok]
        gu = xb @ w_gu[e] + b_gu[e]
        gate, up = jnp.split(gu, 2, axis=-1)
        gate = jnp.minimum(gate, SWIGLU_LIMIT)
        up = jnp.clip(up, -SWIGLU_LIMIT, SWIGLU_LIMIT)
        act = (up + 1) * gate * jax.nn.sigmoid(SWIGLU_ALPHA * gate)
        return act @ w_down[e] + b_down[e]

    yb = lax.map(expert_block, (slot_token.reshape(n_blocks, EXPERT_BLOCK), block_expert))
    y = jnp.zeros_like(x).at[slot_token].add(yb.reshape(-1, d) * slot_weight[:, None].astype(x.dtype))
    return y.reshape(bsz, seq, d)


def setup_inputs(seed: int = 0) -> dict:
    key = jax.random.key(seed)
    ks = iter(jax.random.split(key, 48))
    L, D = DEPTH, D_MODEL

    def nrm(shape, scale):
        return jax.random.normal(next(ks), shape, jnp.float32) * scale

    def unif(shape, lo, hi):
        return jax.random.uniform(next(ks), shape, jnp.float32, lo, hi)

    u = unif((L, LRU_WIDTH), 0.9, 0.999) ** (1.0 / LRU_C)
    lru_lambda = jnp.log(u) - jnp.log1p(-u)
    return {
        'x': nrm((BATCH, SEQ, D), 1.0),
        'c': nrm((BATCH, D), 1.0),
        'w_ada': nrm((L, D, N_ADA * D), 0.02),
        'b_ada': nrm((L, N_ADA * D), 0.01),
        'norm_mix': 1.0 + nrm((L, D), 0.02),
        'norm_moe': 1.0 + nrm((L, D), 0.02),
        'norm_final': 1.0 + nrm((D,), 0.02),
        'w_in': nrm((L, D, N_IN), D ** -0.5),
        'conv_w': nrm((L, CONV_WIDTH, LRU_WIDTH), CONV_WIDTH ** -0.5),
        'conv_b': nrm((L, LRU_WIDTH), 0.01),
        'lru_wa': nrm((L, LRU_BLOCKS, LRU_BLOCK_DIM, LRU_BLOCK_DIM), LRU_BLOCK_DIM ** -0.5),
        'lru_ba': nrm((L, LRU_WIDTH), 0.01),
        'lru_wx': nrm((L, LRU_BLOCKS, LRU_BLOCK_DIM, LRU_BLOCK_DIM), LRU_BLOCK_DIM ** -0.5),
        'lru_bx': nrm((L, LRU_WIDTH), 0.01),
        'lru_lambda': lru_lambda,
        'rw_mu': unif((L, RW_IN), 0.0, 1.0),
        'rw_w0': unif((L, RW_WIDTH), -6.0, -1.0),
        'rw_w_up': nrm((L, DECAY_LORA, RW_WIDTH), 0.1),
        'rw_a0': nrm((L, RW_WIDTH), 0.1),
        'rw_a_up': nrm((L, AAA_LORA, RW_WIDTH), 0.1),
        'rw_g_up': nrm((L, GATE_LORA, RW_WIDTH), GATE_LORA ** -0.5),
        'rw_k_k': 0.85 + nrm((L, RW_WIDTH), 0.02),
        'rw_k_a': 1.0 + nrm((L, RW_WIDTH), 0.02),
        'rw_r_k': nrm((L, RW_HEADS, RW_HEAD_DIM), 0.1),
        'rw_lnx_w': 1.0 + nrm((L, RW_WIDTH), 0.02),
        'rw_lnx_b': nrm((L, RW_WIDTH), 0.01),
        'p_lru': nrm((L, LRU_WIDTH, D), LRU_WIDTH ** -0.5),
        'p_sb': nrm((L, SB_WIDTH, D), SB_WIDTH ** -0.5),
        'p_rwkv': nrm((L, RW_WIDTH, D), RW_WIDTH ** -0.5),
        'w_out': nrm((L, D, D), D ** -0.5),
        'w_router': nrm((L, D, N_EXPERTS), D ** -0.5),
        'b_router': nrm((L, N_EXPERTS), 0.01),
        'w_gu': nrm((L, N_EXPERTS, D, 2 * D_FF), D ** -0.5),
        'b_gu': nrm((L, N_EXPERTS, 2 * D_FF), 0.01),
        'w_down': nrm((L, N_EXPERTS, D_FF, D), D_FF ** -0.5),
        'b_down': nrm((L, N_EXPERTS, D), 0.01),
    }


def reference(x, c, w_ada, b_ada, norm_mix, norm_moe, norm_final, w_in, conv_w, conv_b,
              lru_wa, lru_ba, lru_wx, lru_bx, lru_lambda, rw_mu, rw_w0, rw_w_up, rw_a0, rw_a_up,
              rw_g_up, rw_k_k, rw_k_a, rw_r_k, rw_lnx_w, rw_lnx_b, p_lru, p_sb, p_rwkv, w_out,
              w_router, b_router, w_gu, b_gu, w_down, b_down):
    c_act = jax.nn.silu(c)
    for l in range(DEPTH):
        ada = c_act @ w_ada[l] + b_ada[l]
        sh_mix, sc_mix, g_mix, sh_ffn, sc_ffn, g_ffn = jnp.split(ada, N_ADA, axis=-1)
        h = modulate(rmsnorm(x, norm_mix[l]), sh_mix, sc_mix)
        mix = hybrid_mixer(h, w_in[l], conv_w[l], conv_b[l], lru_wa[l], lru_ba[l], lru_wx[l], lru_bx[l],
                           lru_lambda[l], rw_mu[l], rw_w0[l], rw_w_up[l], rw_a0[l], rw_a_up[l], rw_g_up[l],
                           rw_k_k[l], rw_k_a[l], rw_r_k[l], rw_lnx_w[l], rw_lnx_b[l],
                           p_lru[l], p_sb[l], p_rwkv[l], w_out[l])
        x = x + g_mix[:, None, :] * mix
        h = modulate(rmsnorm(x, norm_moe[l]), sh_ffn, sc_ffn)
        ffn = moe_ffn(h, w_router[l], b_router[l], w_gu[l], b_gu[l], w_down[l], b_down[l])
        x = x + g_ffn[:, None, :] * ffn
    return rmsnorm(x, norm_final)
```

```python
import functools

import jax
import jax.numpy as jnp
import numpy as np
from jax import lax
from jax.experimental import pallas as pl
from jax.experimental.pallas import tpu as pltpu

F32 = jnp.float32
BF16 = jnp.bfloat16

D_MODEL = 1024
DEPTH = 2
NORM_EPS = 1e-6
N_ADA = 6

LRU_BLOCK_DIM = 64
CONV_WIDTH = 4
LRU_C = 8.0

SB_HEADS = 8
SB_HEAD_DIM = 128

RW_HEAD_DIM = 64
RW_HEADS = 16
RW_LORA = 256
RW_GN_EPS = 1e-5 * RW_HEAD_DIM
RW_CHUNK = 64
RW_GROUP = 256
RW_TILE = 256

N_EXPERTS = 32
TOP_K = 4
SWIGLU_LIMIT = 7.0
SWIGLU_ALPHA = 1.702
EXPERT_BLOCK = 256
COMBINE_TOKENS = 64

C_LX, C_LG, C_Q, C_K, C_V, C_RR, C_RK, C_RV, C_GA, C_GB, C_GC, C_LO = (
    0, 1024, 2048, 3072, 4096, 5120, 6144, 7168, 8192, 9216, 10240, 11264)
N_IN = 11520

VMEM_LIMIT_BYTES = 56 * 1024 * 1024
F32_EXP_ZERO = -104.0


def _cparams(sem):
    return pltpu.CompilerParams(dimension_semantics=sem, vmem_limit_bytes=VMEM_LIMIT_BYTES)


def _dot(a, b):
    return jnp.dot(a, b, preferred_element_type=F32)


def _dot_nt(a, b):
    return lax.dot_general(a, b, (((1,), (1,)), ((), ())), preferred_element_type=F32)


def _split(x):
    hi = x.astype(BF16)
    lo = (x - hi.astype(F32)).astype(BF16)
    return hi, lo


def _softplus(x):
    return jnp.maximum(x, 0.0) + jnp.log(1.0 + jnp.exp(-jnp.abs(x)))


def _sigmoid(x):
    return 1.0 / (1.0 + jnp.exp(-x))


def _rmsnorm(x):
    return x * lax.rsqrt(jnp.mean(x * x, axis=-1, keepdims=True) + NORM_EPS)


def _ada_kernel(c_ref, w_ref, b_ref, o_ref):
    c = c_ref[...]
    ca = c * _sigmoid(c)
    o_ref[0] = jnp.dot(ca, w_ref[0], preferred_element_type=F32,
                       precision=lax.Precision.HIGHEST) + b_ref[0]


def _ada(c, w_ada, b_ada):
    depth, d, n = w_ada.shape
    bsz = c.shape[0]
    c8 = jnp.zeros((8, d), F32).at[:bsz].set(c)
    out = pl.pallas_call(
        _ada_kernel,
        grid=(depth, n // d),
        in_specs=[pl.BlockSpec((8, d), lambda l, j: (0, 0)),
                  pl.BlockSpec((1, d, d), lambda l, j: (l, 0, j)),
                  pl.BlockSpec((1, 1, d), lambda l, j: (l, 0, j))],
        out_specs=pl.BlockSpec((1, 8, d), lambda l, j: (l, 0, j)),
        out_shape=jax.ShapeDtypeStruct((depth, 8, n), F32),
        compiler_params=_cparams(("parallel", "parallel")),
        name="ada",
    )(c8, w_ada, b_ada.reshape(depth, 1, n))
    return out[:, :bsz]


def _inproj_kernel(x_ref, g_ref, sh_ref, sc_ref, w_ref, o_ref, h_ref):
    @pl.when(pl.program_id(1) == 0)
    def _():
        h = _rmsnorm(x_ref[...]) * g_ref[...] * (1.0 + sc_ref[0]) + sh_ref[0]
        h_ref[...] = h.astype(BF16)

    o_ref[...] = _dot(h_ref[...], w_ref[...]).astype(BF16)


def _inproj(xf, gain, shift, scale, w_bf16, seq):
    t, d = xf.shape
    n = w_bf16.shape[1]
    tm = min(1024, seq)
    tn = 1280
    per_b = seq // tm
    return pl.pallas_call(
        _inproj_kernel,
        grid=(t // tm, n // tn),
        in_specs=[pl.BlockSpec((tm, d), lambda i, j: (i, 0)),
                  pl.BlockSpec((1, d), lambda i, j: (0, 0)),
                  pl.BlockSpec((1, 1, d), lambda i, j: (i // per_b, 0, 0)),
                  pl.BlockSpec((1, 1, d), lambda i, j: (i // per_b, 0, 0)),
                  pl.BlockSpec((d, tn), lambda i, j: (0, j))],
        out_specs=pl.BlockSpec((tm, tn), lambda i, j: (i, j)),
        out_shape=jax.ShapeDtypeStruct((t, n), BF16),
        scratch_shapes=[pltpu.VMEM((tm, d), BF16)],
        compiler_params=_cparams(("parallel", "arbitrary")),
        name="inproj",
    )(xf, gain.reshape(1, d), shift, scale, w_bf16)


def _rglru_kernel(x_ref, gate_ref, cw_ref, cb_ref, wax_ref, bax_ref, lam_ref, o_ref,
                  tail_ref, h_ref):
    ts, d = x_ref.shape

    @pl.when(pl.program_id(1) == 0)
    def _():
        tail_ref[...] = jnp.zeros_like(tail_ref)
        h_ref[...] = jnp.zeros_like(h_ref)

    x = x_ref[...].astype(F32)
    tail = tail_ref[...]
    row8 = lax.broadcasted_iota(jnp.int32, (8, d), 0)
    xc = cb_ref[...] + cw_ref[0:1, :] * x
    for k in range(1, CONV_WIDTH):
        xs = pltpu.roll(x, k, 0)
        head = jnp.where(row8 < k, pltpu.roll(tail, k, 0), xs[0:8])
        xs = jnp.concatenate([head, xs[8:]], axis=0)
        xc = xc + cw_ref[k:k + 1, :] * xs
    tail_ref[...] = x[ts - 8:]

    rs, is_ = [], []
    for blk in range(d // 128):
        xb = xc[:, blk * 128:(blk + 1) * 128].astype(BF16)
        ri = _dot(xb, wax_ref[blk]) + bax_ref[blk]
        rs.append(ri[:, :128])
        is_.append(ri[:, 128:])
    r = _sigmoid(jnp.concatenate(rs, axis=1))
    i = _sigmoid(jnp.concatenate(is_, axis=1))
    log_a = (-LRU_C) * r * _softplus(-lam_ref[...])
    a = jnp.exp(log_a)
    b = jnp.sqrt(1.0 - jnp.exp(2.0 * log_a)) * (i * xc)

    row = lax.broadcasted_iota(jnp.int32, (ts, d), 0)
    sh = 1
    while sh < ts:
        a_s = pltpu.roll(a, sh, 0)
        b_s = pltpu.roll(b, sh, 0)
        m = row >= sh
        b = jnp.where(m, a * b_s + b, b)
        a = jnp.where(m, a * a_s, a)
        sh *= 2
    h = a * h_ref[...] + b
    h_ref[...] = h[ts - 1:]

    g = gate_ref[...].astype(F32)
    gelu = 0.5 * g * (1.0 + jnp.tanh(0.7978845608028654 * (g + 0.044715 * g * g * g)))
    o_ref[...] = (gelu * h).astype(BF16)


def _rglru(cols, conv_w, conv_b, wa, ba, wx, bx, lam, bsz, seq):
    t = cols.shape[0]
    d = D_MODEL
    ts = min(256, seq)
    per_b = seq // ts
    nb = d // 128

    def bdiag(w):
        w = w.reshape(nb, 2, LRU_BLOCK_DIM, LRU_BLOCK_DIM)
        z = jnp.zeros((nb, LRU_BLOCK_DIM, LRU_BLOCK_DIM), w.dtype)
        top = jnp.concatenate([w[:, 0], z], axis=2)
        bot = jnp.concatenate([z, w[:, 1]], axis=2)
        return jnp.concatenate([top, bot], axis=1)

    wax = jnp.concatenate([bdiag(wa), bdiag(wx)], axis=2).astype(BF16)
    bax = jnp.concatenate([ba.reshape(nb, 1, 128), bx.reshape(nb, 1, 128)], axis=2)
    return pl.pallas_call(
        _rglru_kernel,
        grid=(bsz, per_b),
        in_specs=[pl.BlockSpec((ts, d), lambda b, s: (b * per_b + s, C_LX // d)),
                  pl.BlockSpec((ts, d), lambda b, s: (b * per_b + s, C_LG // d)),
                  pl.BlockSpec((CONV_WIDTH, d), lambda b, s: (0, 0)),
                  pl.BlockSpec((1, d), lambda b, s: (0, 0)),
                  pl.BlockSpec((nb, 128, 256), lambda b, s: (0, 0, 0)),
                  pl.BlockSpec((nb, 1, 256), lambda b, s: (0, 0, 0)),
                  pl.BlockSpec((1, d), lambda b, s: (0, 0))],
        out_specs=pl.BlockSpec((ts, d), lambda b, s: (b * per_b + s, 0)),
        out_shape=jax.ShapeDtypeStruct((t, d), BF16),
        scratch_shapes=[pltpu.VMEM((8, d), F32), pltpu.VMEM((1, d), F32)],
        compiler_params=_cparams(("parallel", "arbitrary")),
        name="rglru",
    )(cols, cols, conv_w, conv_b.reshape(1, d), wax, bax, lam.reshape(1, d))


def _sb_kernel(q_ref, k_ref, v_ref, o_ref, acc_ref, carry_ref):
    tq, dh = q_ref.shape
    qi = pl.program_id(2)
    scale = dh ** -0.5
    q = q_ref[...]
    acc_ref[...] = jnp.zeros_like(acc_ref)
    carry_ref[...] = jnp.zeros_like(carry_ref)
    rowi = lax.broadcasted_iota(jnp.int32, (tq, tq), 0)
    coli = lax.broadcasted_iota(jnp.int32, (tq, tq), 1)
    upper = (rowi > coli).astype(BF16)

    def body(state):
        j, _ = state
        kj = k_ref[pl.ds(pl.multiple_of(j * tq, tq), tq), :]
        vj = v_ref[pl.ds(pl.multiple_of(j * tq, tq), tq), :]
        z = _dot_nt(q, kj) * scale
        sp = _softplus(z)
        valid = jnp.logical_or(j < qi, coli < rowi)
        log_stay = jnp.where(valid, -sp, 0.0)
        hi, lo = _split(log_stay)
        carry = carry_ref[...]
        later = _dot(hi, upper) + _dot(lo, upper) + carry
        attn = jnp.where(valid, jnp.exp(z - sp + later), 0.0)
        acc_ref[...] += _dot(attn.astype(BF16), vj)
        carry = carry + jnp.sum(log_stay, axis=1, keepdims=True)
        carry_ref[...] = carry
        go = jnp.max(carry) >= F32_EXP_ZERO
        return j - 1, go

    def cond(state):
        j, go = state
        return jnp.logical_and(j >= 0, go)

    lax.while_loop(cond, body, (qi, True))
    o_ref[...] = acc_ref[...].astype(BF16)


def _sb_attention(cols, bsz, seq):
    t = cols.shape[0]
    dh = SB_HEAD_DIM
    tq = min(128, seq)
    nq = seq // tq
    qb, kb, vb = C_Q // dh, C_K // dh, C_V // dh
    return pl.pallas_call(
        _sb_kernel,
        grid=(bsz, SB_HEADS, nq),
        in_specs=[pl.BlockSpec((tq, dh), lambda b, h, i: (b * nq + i, qb + h)),
                  pl.BlockSpec((seq, dh), lambda b, h, i: (b, kb + h)),
                  pl.BlockSpec((seq, dh), lambda b, h, i: (b, vb + h))],
        out_specs=pl.BlockSpec((tq, dh), lambda b, h, i: (b * nq + i, h)),
        out_shape=jax.ShapeDtypeStruct((t, SB_HEADS * dh), BF16),
        scratch_shapes=[pltpu.VMEM((tq, dh), F32), pltpu.VMEM((tq, 1), F32)],
        compiler_params=_cparams(("parallel", "parallel", "arbitrary")),
        name="sb_attention",
    )(cols, cols, cols)


def _rwkv_kernel(r_ref, k_ref, v_ref, l_ref, mu_ref, par_ref, lw_ref, o_ref, st_ref, prev_ref):
    ts, gw = r_ref.shape
    c = RW_CHUNK
    nch = ts // c
    n = RW_HEAD_DIM
    nh = gw // n

    @pl.when(pl.program_id(2) == 0)
    def _():
        st_ref[...] = jnp.zeros_like(st_ref)
        prev_ref[...] = jnp.zeros_like(prev_ref)

    row = lax.broadcasted_iota(jnp.int32, (ts, gw), 0)

    def shift_mix(ref, idx):
        x = ref[...].astype(F32)
        prev = jnp.where(row == 0, prev_ref[idx:idx + 1, :], pltpu.roll(x, 1, 0))
        prev_ref[idx:idx + 1, :] = x[ts - 1:]
        return x + mu_ref[idx:idx + 1, :] * (prev - x)

    r = shift_mix(r_ref, 0)
    k = shift_mix(k_ref, 1)
    v = shift_mix(v_ref, 2)
    lo = shift_mix(l_ref, 3)

    w0, a0 = par_ref[0:1, :], par_ref[1:2, :]
    k_k, k_a, r_k = par_ref[2:3, :], par_ref[3:4, :], par_ref[4:5, :]
    lnx_w, lnx_b = par_ref[5:6, :], par_ref[6:7, :]

    dw = _dot(jnp.tanh(lo).astype(BF16), lw_ref[0])
    da = _dot(lo.astype(BF16), lw_ref[1])
    g = _dot(_sigmoid(lo).astype(BF16), lw_ref[2])
    w_log = -_softplus(-(w0 + dw)) - 0.5
    ld = -jnp.exp(w_log)
    rate = _sigmoid(a0 + da)

    ri = lax.broadcasted_iota(jnp.int32, (gw, gw), 0)
    ci = lax.broadcasted_iota(jnp.int32, (gw, gw), 1)
    shift_n = n.bit_length() - 1
    same_blk = (ri >> shift_n) == (ci >> shift_n)
    seg = same_blk.astype(BF16)
    tri = jnp.logical_and(same_blk, ci <= ri).astype(BF16)

    def segsum(x):
        hi, lo_ = _split(x)
        return _dot(hi, seg) + _dot(lo_, seg)

    kk = k * k_k
    kk = kk * lax.rsqrt(jnp.maximum(segsum(kk * kk), 1e-24))
    k2 = k * (1.0 + (rate - 1.0) * k_a)
    av = -kk
    bv = kk * rate

    ld_hi, ld_lo = _split(ld)
    cl = _dot(tri, ld_hi) + _dot(tri, ld_lo)

    t_i = lax.broadcasted_iota(jnp.int32, (c, gw), 0)
    s_i = lax.broadcasted_iota(jnp.int32, (c, gw), 1) & (c - 1)
    strict = s_i < t_i
    incl = s_i <= t_i
    eye = (s_i == t_i).astype(F32)

    def stacked(x):
        xb = x.astype(BF16)
        return jnp.where(same_blk, jnp.concatenate([xb] * nh, axis=0), jnp.zeros((), BF16))

    locals_ = []
    for ch in range(nch):
        sl = slice(ch * c, (ch + 1) * c)
        cl_c, ld_c = cl[sl], ld[sl]
        cl_end = cl_c[c - 1:c]
        e_pos = jnp.exp(cl_c)
        e_neg = jnp.exp(-cl_c)
        e_end = jnp.exp(cl_end - cl_c)
        rt = r[sl] * e_pos
        at = av[sl] * jnp.exp(cl_c - ld_c)
        bt = bv[sl] * e_neg
        kt = k2[sl] * e_neg
        bh = bv[sl] * e_end
        kh = k2[sl] * e_end
        vc = v[sl]
        gam = jnp.exp(cl_end)

        ar = jnp.concatenate([at, rt], axis=0).astype(BF16)
        ab = _dot_nt(ar, stacked(bt))
        ak = _dot_nt(ar, stacked(kt))
        a_m = jnp.where(strict, ab[:c], 0.0)
        pb = jnp.where(incl, ab[c:], 0.0).astype(BF16)
        ak_m = jnp.where(strict, ak[:c], 0.0).astype(BF16)
        pk = jnp.where(incl, ak[c:], 0.0).astype(BF16)

        tinv = eye + a_m
        apow = a_m
        p = 2
        while p < c:
            apow = _dot(apow.astype(BF16), stacked(apow))
            tinv = tinv + _dot(tinv.astype(BF16), stacked(apow))
            p *= 2
        tb = tinv.astype(BF16)

        vs = stacked(vc)
        w = _dot(tb, stacked(at))
        z = _dot(tb, stacked(_dot(ak_m, vs)))
        rhat = rt + _dot(pb, stacked(w))
        yhat = _dot(pb, stacked(z)) + _dot(pk, vs)
        zero = jnp.zeros((c, gw), F32)
        lt = jnp.concatenate([w, z, vc, zero], axis=0).T.astype(BF16)
        x1 = jnp.where(same_blk, _dot(lt, jnp.concatenate([bh, zero, zero, zero], 0).astype(BF16)), 0.0)
        x2 = jnp.where(same_blk, _dot(lt, jnp.concatenate([zero, bh, kh, zero], 0).astype(BF16)), 0.0)
        locals_.append((rhat.astype(BF16), yhat, x1.astype(BF16), x2, gam))

    st = st_ref[...]
    ys = []
    for rhat, yhat, x1, x2, gam in locals_:
        stb = st.astype(BF16)
        ys.append(_dot_nt(rhat, stb) + yhat)
        st = gam * st + _dot(stb, x1) + x2
    st_ref[...] = st
    y = jnp.concatenate(ys, axis=0)

    inv_n = 1.0 / n
    mean = segsum(y) * inv_n
    dev = y - mean
    var = segsum(dev * dev) * inv_n
    yn = dev * lax.rsqrt(var + RW_GN_EPS) * lnx_w + lnx_b
    bonus = segsum(r * k2 * r_k) * v
    o_ref[...] = ((yn + bonus) * g).astype(BF16)


def _rwkv(cols, mu, w0, w_up, a0, a_up, g_up, k_k, k_a, r_k, lnx_w, lnx_b, bsz, seq):
    t = cols.shape[0]
    d = D_MODEL
    gw = RW_GROUP
    ts = min(RW_TILE, seq)
    per_b = seq // ts
    ng = d // gw
    mu4 = jnp.stack([mu[0:d], mu[d:2 * d], mu[2 * d:3 * d],
                     jnp.tile(mu[3 * d:], d // RW_LORA)], axis=0)
    par = jnp.stack([w0, a0, k_k, k_a, r_k.reshape(d), lnx_w, lnx_b, jnp.zeros((d,), F32)], axis=0)
    lw = jnp.zeros((3, RW_LORA, d), F32)
    lw = lw.at[0, 0:64].set(w_up).at[1, 64:128].set(a_up).at[2, 128:256].set(g_up).astype(BF16)
    rb, kb, vb, lb = C_RR // gw, C_RK // gw, C_RV // gw, C_LO // gw
    return pl.pallas_call(
        _rwkv_kernel,
        grid=(bsz, ng, per_b),
        in_specs=[pl.BlockSpec((ts, gw), lambda b, g, s: (b * per_b + s, rb + g)),
                  pl.BlockSpec((ts, gw), lambda b, g, s: (b * per_b + s, kb + g)),
                  pl.BlockSpec((ts, gw), lambda b, g, s: (b * per_b + s, vb + g)),
                  pl.BlockSpec((ts, gw), lambda b, g, s: (b * per_b + s, lb)),
                  pl.BlockSpec((4, gw), lambda b, g, s: (0, g)),
                  pl.BlockSpec((8, gw), lambda b, g, s: (0, g)),
                  pl.BlockSpec((3, RW_LORA, gw), lambda b, g, s: (0, 0, g))],
        out_specs=pl.BlockSpec((ts, gw), lambda b, g, s: (b * per_b + s, g)),
        out_shape=jax.ShapeDtypeStruct((t, d), BF16),
        scratch_shapes=[pltpu.VMEM((gw, gw), F32), pltpu.VMEM((8, gw), F32)],
        compiler_params=_cparams(("parallel", "parallel", "arbitrary")),
        name="rwkv7",
    )(cols, cols, cols, cols, mu4, par, lw)


def _merge_kernel(x_ref, ya_ref, yb_ref, yc_ref, ga_ref, gb_ref, gc_ref, pa_ref, pb_ref, pc_ref,
                  wo_ref, gm_ref, gn_ref, sh_ref, sc_ref, wrh_ref, wrl_ref, br_ref,
                  xo_ref, h_ref, tw_ref, ti_ref):
    m = _sigmoid(ga_ref[...].astype(F32)) * _dot(ya_ref[...], pa_ref[...])
    m = m + _sigmoid(gb_ref[...].astype(F32)) * _dot(yb_ref[...], pb_ref[...])
    m = m + _sigmoid(gc_ref[...].astype(F32)) * _dot(yc_ref[...], pc_ref[...])
    xn = x_ref[...] + gm_ref[0] * _dot(m.astype(BF16), wo_ref[...])
    xo_ref[...] = xn
    h = _rmsnorm(xn) * gn_ref[...] * (1.0 + sc_ref[0]) + sh_ref[0]
    h_ref[...] = h

    hh, hl = _split(h)
    logits = _dot(hh, wrh_ref[...]) + _dot(hh, wrl_ref[...]) + _dot(hl, wrh_ref[...]) + br_ref[...]
    lane = lax.broadcasted_iota(jnp.int32, logits.shape, 1)
    lanes = logits.shape[1]
    tw = jnp.zeros(logits.shape, F32)
    ti = jnp.zeros(logits.shape, jnp.int32)
    top = None
    wsum = None
    es = []
    for kth in range(TOP_K):
        mk = jnp.max(logits, axis=1, keepdims=True)
        ik = jnp.min(jnp.where(logits == mk, lane, lanes), axis=1, keepdims=True)
        logits = jnp.where(lane == ik, -jnp.inf, logits)
        if kth == 0:
            top = mk
        e = jnp.exp(mk - top)
        es.append(e)
        wsum = e if wsum is None else wsum + e
        ti = jnp.where(lane == kth, ik, ti)
    inv = 1.0 / wsum
    for kth in range(TOP_K):
        tw = jnp.where(lane == kth, es[kth] * inv, tw)
    tw_ref[...] = tw
    ti_ref[...] = ti


def _merge(xf, ya, yb, yc, cols, p_lru, p_sb, p_rwkv, w_out, g_mix, gain, shift, scale,
           w_router, b_router, seq):
    t, d = xf.shape
    tm = min(256, seq)
    per_b = seq // tm
    ne = w_router.shape[1]
    wr = jnp.zeros((d, 128), F32).at[:, :ne].set(w_router)
    wrh, wrl = _split(wr)
    br = jnp.full((1, 128), -1e30, F32).at[0, :ne].set(b_router)
    row = lambda i: (i, 0)
    const = lambda i: (0, 0)
    perb = lambda i: (i // per_b, 0, 0)
    return pl.pallas_call(
        _merge_kernel,
        grid=(t // tm,),
        in_specs=[pl.BlockSpec((tm, d), row), pl.BlockSpec((tm, d), row),
                  pl.BlockSpec((tm, d), row), pl.BlockSpec((tm, d), row),
                  pl.BlockSpec((tm, d), lambda i: (i, C_GA // d)),
                  pl.BlockSpec((tm, d), lambda i: (i, C_GB // d)),
                  pl.BlockSpec((tm, d), lambda i: (i, C_GC // d)),
                  pl.BlockSpec((d, d), const), pl.BlockSpec((d, d), const),
                  pl.BlockSpec((d, d), const), pl.BlockSpec((d, d), const),
                  pl.BlockSpec((1, 1, d), perb), pl.BlockSpec((1, d), const),
                  pl.BlockSpec((1, 1, d), perb), pl.BlockSpec((1, 1, d), perb),
                  pl.BlockSpec((d, 128), const), pl.BlockSpec((d, 128), const),
                  pl.BlockSpec((1, 128), const)],
        out_specs=[pl.BlockSpec((tm, d), row), pl.BlockSpec((tm, d), row),
                   pl.BlockSpec((tm, 128), row), pl.BlockSpec((tm, 128), row)],
        out_shape=[jax.ShapeDtypeStruct((t, d), F32), jax.ShapeDtypeStruct((t, d), F32),
                   jax.ShapeDtypeStruct((t, 128), F32), jax.ShapeDtypeStruct((t, 128), jnp.int32)],
        compiler_params=_cparams(("parallel",)),
        name="merge_router",
    )(xf, ya, yb, yc, cols, cols, cols, p_lru.astype(BF16), p_sb.astype(BF16), p_rwkv.astype(BF16),
      w_out.astype(BF16), g_mix, gain.reshape(1, d), shift, scale, wrh, wrl, br)


def _row_copy(src_hbm, dst_vmem, sem, src_row, dst_row):
    return pltpu.make_async_copy(src_hbm.at[pl.ds(src_row, 1)], dst_vmem.at[pl.ds(dst_row, 1)], sem)


def _gather_rows(idx_ref, src_hbm, dst_vmem, sem, n_rows):
    def start(i, carry):
        _row_copy(src_hbm, dst_vmem, sem, idx_ref[0, 0, i], i).start()
        return carry

    lax.fori_loop(0, n_rows, start, 0, unroll=8)

    def wait(i, carry):
        _row_copy(src_hbm, dst_vmem, sem, 0, i).wait()
        return carry

    lax.fori_loop(0, n_rows, wait, 0, unroll=8)


def _expert_kernel(be_ref, nused_ref, tok_ref, h_hbm, sw_ref, wgu_ref, bgu_ref, wd_ref, bd_ref,
                   o_ref, xbuf, sem):
    blk = pl.program_id(0)
    rows = xbuf.shape[0]
    dff = wd_ref.shape[1]

    @pl.when(blk < nused_ref[0])
    def _():
        _gather_rows(tok_ref, h_hbm, xbuf, sem, rows)
        xb = xbuf[...].astype(BF16)
        gu = _dot(xb, wgu_ref[0]) + bgu_ref[0]
        gate = jnp.minimum(gu[:, :dff], SWIGLU_LIMIT)
        up = jnp.clip(gu[:, dff:], -SWIGLU_LIMIT, SWIGLU_LIMIT)
        act = (up + 1.0) * gate * _sigmoid(SWIGLU_ALPHA * gate)
        y = _dot(act.astype(BF16), wd_ref[0]) + bd_ref[0]
        o_ref[...] = y * sw_ref[...]

    @pl.when(blk >= nused_ref[0])
    def _():
        o_ref[...] = jnp.zeros_like(o_ref)


def _experts(hffn, slot_token, slot_weight, block_expert, n_used, w_gu, b_gu, w_down, b_down):
    t, d = hffn.shape
    n_blocks = block_expert.shape[0]
    ne, _, dff2 = w_gu.shape
    dff = dff2 // 2
    eb = EXPERT_BLOCK
    grid_spec = pltpu.PrefetchScalarGridSpec(
        num_scalar_prefetch=2,
        grid=(n_blocks,),
        in_specs=[pl.BlockSpec((1, 1, eb), lambda b, be, nu: (b, 0, 0), memory_space=pltpu.SMEM),
                  pl.BlockSpec(memory_space=pl.ANY),
                  pl.BlockSpec((eb, 1), lambda b, be, nu: (b, 0)),
                  pl.BlockSpec((1, d, dff2), lambda b, be, nu: (be[b], 0, 0)),
                  pl.BlockSpec((1, 1, dff2), lambda b, be, nu: (be[b], 0, 0)),
                  pl.BlockSpec((1, dff, d), lambda b, be, nu: (be[b], 0, 0)),
                  pl.BlockSpec((1, 1, d), lambda b, be, nu: (be[b], 0, 0))],
        out_specs=pl.BlockSpec((eb, d), lambda b, be, nu: (b, 0)),
        scratch_shapes=[pltpu.VMEM((eb, d), F32), pltpu.SemaphoreType.DMA(())],
    )
    return pl.pallas_call(
        _expert_kernel,
        grid_spec=grid_spec,
        out_shape=jax.ShapeDtypeStruct((n_blocks * eb, d), F32),
        compiler_params=_cparams(("arbitrary",)),
        name="moe_experts",
    )(block_expert, n_used, slot_token.reshape(n_blocks, 1, eb), hffn,
      slot_weight.reshape(n_blocks * eb, 1), w_gu.astype(BF16), b_gu.reshape(ne, 1, dff2),
      w_down.astype(BF16), b_down.reshape(ne, 1, d))


def _combine_kernel(slot_ref, yb_hbm, x_ref, gf_ref, gain_ref, o_ref, buf, sem, *, final):
    tt = x_ref.shape[0]
    _gather_rows(slot_ref, yb_hbm, buf, sem, TOP_K * tt)
    ffn = buf[0:tt, :]
    for kth in range(1, TOP_K):
        ffn = ffn + buf[kth * tt:(kth + 1) * tt, :]
    xn = x_ref[...] + gf_ref[0] * ffn
    if final:
        xn = _rmsnorm(xn) * gain_ref[...]
    o_ref[...] = xn


def _combine(xf, ybuf, tok_slots, g_ffn, final_gain, seq, final):
    t, d = xf.shape
    tt = min(COMBINE_TOKENS, seq)
    per_b = seq // tt
    nt = t // tt
    slots = tok_slots.reshape(nt, tt, TOP_K).transpose(0, 2, 1).reshape(nt, 1, TOP_K * tt)
    return pl.pallas_call(
        functools.partial(_combine_kernel, final=final),
        grid=(nt,),
        in_specs=[pl.BlockSpec((1, 1, TOP_K * tt), lambda i: (i, 0, 0), memory_space=pltpu.SMEM),
                  pl.BlockSpec(memory_space=pl.ANY),
                  pl.BlockSpec((tt, d), lambda i: (i, 0)),
                  pl.BlockSpec((1, 1, d), lambda i: (i // per_b, 0, 0)),
                  pl.BlockSpec((1, d), lambda i: (0, 0))],
        out_specs=pl.BlockSpec((tt, d), lambda i: (i, 0)),
        out_shape=jax.ShapeDtypeStruct((t, d), F32),
        scratch_shapes=[pltpu.VMEM((TOP_K * tt, d), F32), pltpu.SemaphoreType.DMA(())],
        compiler_params=_cparams(("arbitrary",)),
        name="moe_combine",
    )(slots, ybuf, xf, g_ffn, final_gain.reshape(1, d))


def _route_plan(top_idx, top_w):
    n_tok = top_idx.shape[0]
    n_assign = n_tok * TOP_K
    flat_e = top_idx.reshape(-1)
    order = jnp.argsort(flat_e)
    sorted_e = flat_e[order]
    counts = jnp.bincount(flat_e, length=N_EXPERTS)
    padded = (counts + EXPERT_BLOCK - 1) // EXPERT_BLOCK * EXPERT_BLOCK
    pad_end = jnp.cumsum(padded)
    pad_start = pad_end - padded
    start = jnp.cumsum(counts) - counts
    slot = (pad_start[sorted_e] + jnp.arange(n_assign) - start[sorted_e]).astype(jnp.int32)
    n_blocks = -(-n_assign // EXPERT_BLOCK) + N_EXPERTS
    n_slots = n_blocks * EXPERT_BLOCK
    slot_token = jnp.zeros((n_slots,), jnp.int32).at[slot].set((order // TOP_K).astype(jnp.int32))
    slot_weight = jnp.zeros((n_slots,), F32).at[slot].set(top_w.reshape(-1)[order])
    block_expert = jnp.minimum(
        jnp.searchsorted(pad_end, jnp.arange(n_blocks) * EXPERT_BLOCK, side='right'),
        N_EXPERTS - 1).astype(jnp.int32)
    tok_slots = jnp.zeros((n_assign,), jnp.int32).at[order].set(slot).reshape(n_tok, TOP_K)
    n_used = (pad_end[-1] // EXPERT_BLOCK).astype(jnp.int32).reshape(1)
    return slot_token, slot_weight, block_expert, n_used, tok_slots


def _in_weight(w_in_l):
    lo0 = C_GA
    return jnp.concatenate([w_in_l[:, :lo0], w_in_l[:, lo0 + RW_LORA:], w_in_l[:, lo0:lo0 + RW_LORA]],
                           axis=1).astype(BF16)


def kernel(x, c, w_ada, b_ada, norm_mix, norm_moe, norm_final, w_in, conv_w, conv_b, lru_wa, lru_ba, lru_wx, lru_bx, lru_lambda, rw_mu, rw_w0, rw_w_up, rw_a0, rw_a_up, rw_g_up, rw_k_k, rw_k_a, rw_r_k, rw_lnx_w, rw_lnx_b, p_lru, p_sb, p_rwkv, w_out, w_router, b_router, w_gu, b_gu, w_down, b_down):
    bsz, seq, d = x.shape
    depth = w_in.shape[0]
    xf = x.reshape(bsz * seq, d)
    ada = _ada(c, w_ada, b_ada)
    for l in range(depth):
        sh_mix, sc_mix, g_mix, sh_ffn, sc_ffn, g_ffn = [
            ada[l, :, i * d:(i + 1) * d].reshape(bsz, 1, d) for i in range(N_ADA)]
        cols = _inproj(xf, norm_mix[l], sh_mix, sc_mix, _in_weight(w_in[l]), seq)
        ya = _rglru(cols, conv_w[l], conv_b[l], lru_wa[l], lru_ba[l], lru_wx[l], lru_bx[l],
                    lru_lambda[l], bsz, seq)
        yb = _sb_attention(cols, bsz, seq)
        yc = _rwkv(cols, rw_mu[l], rw_w0[l], rw_w_up[l], rw_a0[l], rw_a_up[l], rw_g_up[l],
                   rw_k_k[l], rw_k_a[l], rw_r_k[l], rw_lnx_w[l], rw_lnx_b[l], bsz, seq)
        xf, hffn, tw, ti = _merge(xf, ya, yb, yc, cols, p_lru[l], p_sb[l], p_rwkv[l], w_out[l],
                                  g_mix, norm_moe[l], sh_ffn, sc_ffn, w_router[l], b_router[l], seq)
        slot_token, slot_weight, block_expert, n_used, tok_slots = _route_plan(
            ti[:, :TOP_K], tw[:, :TOP_K])
        ybuf = _experts(hffn, slot_token, slot_weight, block_expert, n_used,
                        w_gu[l], b_gu[l], w_down[l], b_down[l])
        xf = _combine(xf, ybuf, tok_slots, g_ffn, norm_final, seq, final=(l == depth - 1))
    return xf.reshape(bsz, seq, d)
```

```python
import functools

import jax
import jax.numpy as jnp
import numpy as np
from jax import lax
from jax.experimental import pallas as pl
from jax.experimental.pallas import tpu as pltpu

F32 = jnp.float32
BF16 = jnp.bfloat16

D_MODEL = 1024
DEPTH = 2
NORM_EPS = 1e-6
N_ADA = 6

LRU_BLOCK_DIM = 64
CONV_WIDTH = 4
LRU_C = 8.0

SB_HEADS = 8
SB_HEAD_DIM = 128

RW_HEAD_DIM = 64
RW_HEADS = 16
RW_LORA = 256
RW_GN_EPS = 1e-5 * RW_HEAD_DIM
RW_CHUNK = 64
RW_GROUP = 256
RW_TILE = 256

N_EXPERTS = 32
TOP_K = 4
SWIGLU_LIMIT = 7.0
SWIGLU_ALPHA = 1.702
EXPERT_BLOCK = 256
DISPATCH_TOKENS = 256
COMBINE_TOKENS = 128

C_LX, C_LG, C_Q, C_K, C_V, C_RR, C_RK, C_RV, C_GA, C_GB, C_GC, C_LO = (
    0, 1024, 2048, 3072, 4096, 5120, 6144, 7168, 8192, 9216, 10240, 11264)
N_IN = 11520

VMEM_LIMIT_BYTES = 56 * 1024 * 1024
F32_EXP_ZERO = -104.0


def _cparams(sem):
    return pltpu.CompilerParams(dimension_semantics=sem, vmem_limit_bytes=VMEM_LIMIT_BYTES)


def _dot(a, b):
    return jnp.dot(a, b, preferred_element_type=F32)


def _dot_nt(a, b):
    return lax.dot_general(a, b, (((1,), (1,)), ((), ())), preferred_element_type=F32)


def _split(x):
    hi = x.astype(BF16)
    lo = (x - hi.astype(F32)).astype(BF16)
    return hi, lo


def _softplus(x):
    return jnp.maximum(x, 0.0) + jnp.log(1.0 + jnp.exp(-jnp.abs(x)))


def _sigmoid(x):
    return 1.0 / (1.0 + jnp.exp(-x))


def _rmsnorm(x):
    return x * lax.rsqrt(jnp.mean(x * x, axis=-1, keepdims=True) + NORM_EPS)


def _ada_kernel(c_ref, w_ref, b_ref, o_ref):
    c = c_ref[...]
    ca = c * _sigmoid(c)
    o_ref[0] = jnp.dot(ca, w_ref[0], preferred_element_type=F32,
                       precision=lax.Precision.HIGHEST) + b_ref[0]


def _ada(c, w_ada, b_ada):
    depth, d, n = w_ada.shape
    bsz = c.shape[0]
    c8 = jnp.zeros((8, d), F32).at[:bsz].set(c)
    out = pl.pallas_call(
        _ada_kernel,
        grid=(depth, n // d),
        in_specs=[pl.BlockSpec((8, d), lambda l, j: (0, 0)),
                  pl.BlockSpec((1, d, d), lambda l, j: (l, 0, j)),
                  pl.BlockSpec((1, 1, d), lambda l, j: (l, 0, j))],
        out_specs=pl.BlockSpec((1, 8, d), lambda l, j: (l, 0, j)),
        out_shape=jax.ShapeDtypeStruct((depth, 8, n), F32),
        compiler_params=_cparams(("parallel", "parallel")),
        name="ada",
    )(c8, w_ada, b_ada.reshape(depth, 1, n))
    return out[:, :bsz]


def _inproj_kernel(x_ref, g_ref, sh_ref, sc_ref, w_ref, o_ref, h_ref):
    @pl.when(pl.program_id(1) == 0)
    def _():
        h = _rmsnorm(x_ref[...]) * g_ref[...] * (1.0 + sc_ref[0]) + sh_ref[0]
        h_ref[...] = h.astype(BF16)

    o_ref[...] = _dot(h_ref[...], w_ref[...]).astype(BF16)


def _inproj(xf, gain, shift, scale, w_bf16, seq):
    t, d = xf.shape
    n = w_bf16.shape[1]
    tm = min(1024, seq)
    tn = 1280
    per_b = seq // tm
    return pl.pallas_call(
        _inproj_kernel,
        grid=(t // tm, n // tn),
        in_specs=[pl.BlockSpec((tm, d), lambda i, j: (i, 0)),
                  pl.BlockSpec((1, d), lambda i, j: (0, 0)),
                  pl.BlockSpec((1, 1, d), lambda i, j: (i // per_b, 0, 0)),
                  pl.BlockSpec((1, 1, d), lambda i, j: (i // per_b, 0, 0)),
                  pl.BlockSpec((d, tn), lambda i, j: (0, j))],
        out_specs=pl.BlockSpec((tm, tn), lambda i, j: (i, j)),
        out_shape=jax.ShapeDtypeStruct((t, n), BF16),
        scratch_shapes=[pltpu.VMEM((tm, d), BF16)],
        compiler_params=_cparams(("parallel", "arbitrary")),
        name="inproj",
    )(xf, gain.reshape(1, d), shift, scale, w_bf16)


def _rglru_kernel(x_ref, gate_ref, cw_ref, cb_ref, wax_ref, bax_ref, lam_ref, o_ref,
                  tail_ref, h_ref):
    ts, d = x_ref.shape

    @pl.when(pl.program_id(1) == 0)
    def _():
        tail_ref[...] = jnp.zeros_like(tail_ref)
        h_ref[...] = jnp.zeros_like(h_ref)

    x = x_ref[...].astype(F32)
    tail = tail_ref[...]
    row8 = lax.broadcasted_iota(jnp.int32, (8, d), 0)
    xc = cb_ref[...] + cw_ref[0:1, :] * x
    for k in range(1, CONV_WIDTH):
        xs = pltpu.roll(x, k, 0)
        head = jnp.where(row8 < k, pltpu.roll(tail, k, 0), xs[0:8])
        xs = jnp.concatenate([head, xs[8:]], axis=0)
        xc = xc + cw_ref[k:k + 1, :] * xs
    tail_ref[...] = x[ts - 8:]

    rs, is_ = [], []
    for blk in range(d // 128):
        xb = xc[:, blk * 128:(blk + 1) * 128].astype(BF16)
        ri = _dot(xb, wax_ref[blk]) + bax_ref[blk]
        rs.append(ri[:, :128])
        is_.append(ri[:, 128:])
    r = _sigmoid(jnp.concatenate(rs, axis=1))
    i = _sigmoid(jnp.concatenate(is_, axis=1))
    log_a = (-LRU_C) * r * _softplus(-lam_ref[...])
    a = jnp.exp(log_a)
    b = jnp.sqrt(1.0 - jnp.exp(2.0 * log_a)) * (i * xc)

    row = lax.broadcasted_iota(jnp.int32, (ts, d), 0)
    sh = 1
    while sh < ts:
        a_s = pltpu.roll(a, sh, 0)
        b_s = pltpu.roll(b, sh, 0)
        m = row >= sh
        b = jnp.where(m, a * b_s + b, b)
        a = jnp.where(m, a * a_s, a)
        sh *= 2
    h = a * h_ref[...] + b
    h_ref[...] = h[ts - 1:]

    g = gate_ref[...].astype(F32)
    gelu = 0.5 * g * (1.0 + jnp.tanh(0.7978845608028654 * (g + 0.044715 * g * g * g)))
    o_ref[...] = (gelu * h).astype(BF16)


def _rglru(cols, conv_w, conv_b, wa, ba, wx, bx, lam, bsz, seq):
    t = cols.shape[0]
    d = D_MODEL
    ts = min(256, seq)
    per_b = seq // ts
    nb = d // 128

    def bdiag(w):
        w = w.reshape(nb, 2, LRU_BLOCK_DIM, LRU_BLOCK_DIM)
        z = jnp.zeros((nb, LRU_BLOCK_DIM, LRU_BLOCK_DIM), w.dtype)
        top = jnp.concatenate([w[:, 0], z], axis=2)
        bot = jnp.concatenate([z, w[:, 1]], axis=2)
        return jnp.concatenate([top, bot], axis=1)

    wax = jnp.concatenate([bdiag(wa), bdiag(wx)], axis=2).astype(BF16)
    bax = jnp.concatenate([ba.reshape(nb, 1, 128), bx.reshape(nb, 1, 128)], axis=2)
    return pl.pallas_call(
        _rglru_kernel,
        grid=(bsz, per_b),
        in_specs=[pl.BlockSpec((ts, d), lambda b, s: (b * per_b + s, C_LX // d)),
                  pl.BlockSpec((ts, d), lambda b, s: (b * per_b + s, C_LG // d)),
                  pl.BlockSpec((CONV_WIDTH, d), lambda b, s: (0, 0)),
                  pl.BlockSpec((1, d), lambda b, s: (0, 0)),
                  pl.BlockSpec((nb, 128, 256), lambda b, s: (0, 0, 0)),
                  pl.BlockSpec((nb, 1, 256), lambda b, s: (0, 0, 0)),
                  pl.BlockSpec((1, d), lambda b, s: (0, 0))],
        out_specs=pl.BlockSpec((ts, d), lambda b, s: (b * per_b + s, 0)),
        out_shape=jax.ShapeDtypeStruct((t, d), BF16),
        scratch_shapes=[pltpu.VMEM((8, d), F32), pltpu.VMEM((1, d), F32)],
        compiler_params=_cparams(("parallel", "arbitrary")),
        name="rglru",
    )(cols, cols, conv_w, conv_b.reshape(1, d), wax, bax, lam.reshape(1, d))


def _sb_kernel(q_ref, k_ref, v_ref, o_ref, acc_ref, carry_ref):
    tq = q_ref.shape[0]
    dh = SB_HEAD_DIM
    nh = q_ref.shape[1] // dh
    heads = range(nh)
    qi = pl.program_id(2)
    scale = dh ** -0.5
    acc_ref[...] = jnp.zeros_like(acc_ref)
    carry_ref[...] = jnp.zeros_like(carry_ref)
    rowi = lax.broadcasted_iota(jnp.int32, (tq, tq), 0)
    coli = lax.broadcasted_iota(jnp.int32, (tq, tq), 1)
    upper = (rowi > coli).astype(BF16)

    def body(state):
        j, _ = state
        rows = pl.ds(pl.multiple_of(j * tq, tq), tq)
        lanes = [slice(h * dh, (h + 1) * dh) for h in heads]
        valid = jnp.logical_or(j < qi, coli < rowi)
        z = [_dot_nt(q_ref[:, lanes[h]], k_ref[rows, lanes[h]]) * scale for h in heads]
        sp = [_softplus(z[h]) for h in heads]
        log_stay = [jnp.where(valid, -sp[h], 0.0) for h in heads]
        parts = [_split(log_stay[h]) for h in heads]
        carry = [carry_ref[h] for h in heads]
        later = [_dot(parts[h][0], upper) + _dot(parts[h][1], upper) + carry[h] for h in heads]
        attn = [jnp.where(valid, jnp.exp(z[h] - sp[h] + later[h]), 0.0).astype(BF16) for h in heads]
        for h in heads:
            acc_ref[:, lanes[h]] += _dot(attn[h], v_ref[rows, lanes[h]])
        carry = [carry[h] + jnp.sum(log_stay[h], axis=1, keepdims=True) for h in heads]
        top = carry[0]
        for h in heads:
            carry_ref[h] = carry[h]
            top = jnp.maximum(top, carry[h])
        go = jnp.max(top) >= F32_EXP_ZERO
        return j - 1, go

    def cond(state):
        j, go = state
        return jnp.logical_and(j >= 0, go)

    lax.while_loop(cond, body, (qi, True))
    o_ref[...] = acc_ref[...].astype(BF16)


def _sb_attention(cols, bsz, seq):
    t = cols.shape[0]
    dh = SB_HEAD_DIM
    hg = 4
    gw = hg * dh
    tq = min(128, seq)
    nq = seq // tq
    qb, kb, vb = C_Q // gw, C_K // gw, C_V // gw
    return pl.pallas_call(
        _sb_kernel,
        grid=(bsz, SB_HEADS // hg, nq),
        in_specs=[pl.BlockSpec((tq, gw), lambda b, g, i: (b * nq + i, qb + g)),
                  pl.BlockSpec((seq, gw), lambda b, g, i: (b, kb + g)),
                  pl.BlockSpec((seq, gw), lambda b, g, i: (b, vb + g))],
        out_specs=pl.BlockSpec((tq, gw), lambda b, g, i: (b * nq + i, g)),
        out_shape=jax.ShapeDtypeStruct((t, SB_HEADS * dh), BF16),
        scratch_shapes=[pltpu.VMEM((tq, gw), F32), pltpu.VMEM((hg, tq, 1), F32)],
        compiler_params=_cparams(("parallel", "parallel", "arbitrary")),
        name="sb_attention",
    )(cols, cols, cols)


def _rwkv_kernel(r_ref, k_ref, v_ref, l_ref, mu_ref, par_ref, lw_ref, o_ref, st_ref, prev_ref):
    ts, gw = r_ref.shape
    c = RW_CHUNK
    nch = ts // c
    n = RW_HEAD_DIM
    nh = gw // n

    @pl.when(pl.program_id(2) == 0)
    def _():
        st_ref[...] = jnp.zeros_like(st_ref)
        prev_ref[...] = jnp.zeros_like(prev_ref)

    row = lax.broadcasted_iota(jnp.int32, (ts, gw), 0)

    def shift_mix(ref, idx):
        x = ref[...].astype(F32)
        prev = jnp.where(row == 0, prev_ref[idx:idx + 1, :], pltpu.roll(x, 1, 0))
        prev_ref[idx:idx + 1, :] = x[ts - 1:]
        return x + mu_ref[idx:idx + 1, :] * (prev - x)

    r = shift_mix(r_ref, 0)
    k = shift_mix(k_ref, 1)
    v = shift_mix(v_ref, 2)
    lo = shift_mix(l_ref, 3)

    w0, a0 = par_ref[0:1, :], par_ref[1:2, :]
    k_k, k_a, r_k = par_ref[2:3, :], par_ref[3:4, :], par_ref[4:5, :]
    lnx_w, lnx_b = par_ref[5:6, :], par_ref[6:7, :]

    dw = _dot(jnp.tanh(lo).astype(BF16), lw_ref[0])
    da = _dot(lo.astype(BF16), lw_ref[1])
    g = _dot(_sigmoid(lo).astype(BF16), lw_ref[2])
    w_log = -_softplus(-(w0 + dw)) - 0.5
    ld = -jnp.exp(w_log)
    rate = _sigmoid(a0 + da)

    ri = lax.broadcasted_iota(jnp.int32, (gw, gw), 0)
    ci = lax.broadcasted_iota(jnp.int32, (gw, gw), 1)
    shift_n = n.bit_length() - 1
    same_blk = (ri >> shift_n) == (ci >> shift_n)
    seg = same_blk.astype(BF16)
    tri = jnp.logical_and(same_blk, ci <= ri).astype(BF16)

    def segsum(x):
        hi, lo_ = _split(x)
        return _dot(hi, seg) + _dot(lo_, seg)

    kk = k * k_k
    kk = kk * lax.rsqrt(jnp.maximum(segsum(kk * kk), 1e-24))
    k2 = k * (1.0 + (rate - 1.0) * k_a)
    av = -kk
    bv = kk * rate

    ld_hi, ld_lo = _split(ld)
    cl = _dot(tri, ld_hi) + _dot(tri, ld_lo)

    t_i = lax.broadcasted_iota(jnp.int32, (c, gw), 0)
    s_i = lax.broadcasted_iota(jnp.int32, (c, gw), 1) & (c - 1)
    strict = s_i < t_i
    incl = s_i <= t_i
    eye = (s_i == t_i).astype(F32)

    def stacked(x):
        xb = x.astype(BF16)
        return jnp.where(same_blk, jnp.concatenate([xb] * nh, axis=0), jnp.zeros((), BF16))

    chs = range(nch)
    sls = [slice(ch * c, (ch + 1) * c) for ch in chs]
    cl_c = [cl[sl] for sl in sls]
    cl_end = [x[c - 1:c] for x in cl_c]
    e_pos = [jnp.exp(x) for x in cl_c]
    e_neg = [jnp.exp(-x) for x in cl_c]
    e_end = [jnp.exp(cl_end[ch] - cl_c[ch]) for ch in chs]
    rt = [r[sls[ch]] * e_pos[ch] for ch in chs]
    at = [av[sls[ch]] * jnp.exp(cl_c[ch] - ld[sls[ch]]) for ch in chs]
    bt = [bv[sls[ch]] * e_neg[ch] for ch in chs]
    kt = [k2[sls[ch]] * e_neg[ch] for ch in chs]
    bh = [bv[sls[ch]] * e_end[ch] for ch in chs]
    kh = [k2[sls[ch]] * e_end[ch] for ch in chs]
    vc = [v[sl] for sl in sls]
    gam = [jnp.exp(x) for x in cl_end]

    ar = [jnp.concatenate([at[ch], rt[ch]], axis=0).astype(BF16) for ch in chs]
    ab = [_dot_nt(ar[ch], stacked(bt[ch])) for ch in chs]
    ak = [_dot_nt(ar[ch], stacked(kt[ch])) for ch in chs]
    a_m = [jnp.where(strict, x[:c], 0.0) for x in ab]
    pb = [jnp.where(incl, x[c:], 0.0).astype(BF16) for x in ab]
    ak_m = [jnp.where(strict, x[:c], 0.0).astype(BF16) for x in ak]
    pk = [jnp.where(incl, x[c:], 0.0).astype(BF16) for x in ak]

    tinv = [eye + x for x in a_m]
    apow = a_m
    p = 2
    while p < c:
        apow = [_dot(x.astype(BF16), stacked(x)) for x in apow]
        tinv = [tinv[ch] + _dot(tinv[ch].astype(BF16), stacked(apow[ch])) for ch in chs]
        p *= 2
    tb = [x.astype(BF16) for x in tinv]

    vs = [stacked(x) for x in vc]
    w = [_dot(tb[ch], stacked(at[ch])) for ch in chs]
    akv = [_dot(ak_m[ch], vs[ch]) for ch in chs]
    z = [_dot(tb[ch], stacked(akv[ch])) for ch in chs]
    rhat = [(rt[ch] + _dot(pb[ch], stacked(w[ch]))).astype(BF16) for ch in chs]
    yhat = [_dot(pb[ch], stacked(z[ch])) + _dot(pk[ch], vs[ch]) for ch in chs]
    zero = jnp.zeros((c, gw), F32)
    lt = [jnp.concatenate([w[ch], z[ch], vc[ch], zero], axis=0).T.astype(BF16) for ch in chs]
    x1 = [jnp.where(same_blk, _dot(lt[ch], jnp.concatenate([bh[ch], zero, zero, zero], 0).astype(BF16)),
                    0.0).astype(BF16) for ch in chs]
    x2 = [jnp.where(same_blk, _dot(lt[ch], jnp.concatenate([zero, bh[ch], kh[ch], zero], 0).astype(BF16)),
                    0.0) for ch in chs]

    st = st_ref[...]
    ys = []
    for ch in chs:
        stb = st.astype(BF16)
        ys.append(_dot_nt(rhat[ch], stb) + yhat[ch])
        st = gam[ch] * st + _dot(stb, x1[ch]) + x2[ch]
    st_ref[...] = st
    y = jnp.concatenate(ys, axis=0)

    inv_n = 1.0 / n
    mean = segsum(y) * inv_n
    dev = y - mean
    var = segsum(dev * dev) * inv_n
    yn = dev * lax.rsqrt(var + RW_GN_EPS) * lnx_w + lnx_b
    bonus = segsum(r * k2 * r_k) * v
    o_ref[...] = ((yn + bonus) * g).astype(BF16)


def _rwkv(cols, mu, w0, w_up, a0, a_up, g_up, k_k, k_a, r_k, lnx_w, lnx_b, bsz, seq):
    t = cols.shape[0]
    d = D_MODEL
    gw = RW_GROUP
    ts = min(RW_TILE, seq)
    per_b = seq // ts
    ng = d // gw
    mu4 = jnp.stack([mu[0:d], mu[d:2 * d], mu[2 * d:3 * d],
                     jnp.tile(mu[3 * d:], d // RW_LORA)], axis=0)
    par = jnp.stack([w0, a0, k_k, k_a, r_k.reshape(d), lnx_w, lnx_b, jnp.zeros((d,), F32)], axis=0)
    lw = jnp.zeros((3, RW_LORA, d), F32)
    lw = lw.at[0, 0:64].set(w_up).at[1, 64:128].set(a_up).at[2, 128:256].set(g_up).astype(BF16)
    rb, kb, vb, lb = C_RR // gw, C_RK // gw, C_RV // gw, C_LO // gw
    return pl.pallas_call(
        _rwkv_kernel,
        grid=(bsz, ng, per_b),
        in_specs=[pl.BlockSpec((ts, gw), lambda b, g, s: (b * per_b + s, rb + g)),
                  pl.BlockSpec((ts, gw), lambda b, g, s: (b * per_b + s, kb + g)),
                  pl.BlockSpec((ts, gw), lambda b, g, s: (b * per_b + s, vb + g)),
                  pl.BlockSpec((ts, gw), lambda b, g, s: (b * per_b + s, lb)),
                  pl.BlockSpec((4, gw), lambda b, g, s: (0, g)),
                  pl.BlockSpec((8, gw), lambda b, g, s: (0, g)),
                  pl.BlockSpec((3, RW_LORA, gw), lambda b, g, s: (0, 0, g))],
        out_specs=pl.BlockSpec((ts, gw), lambda b, g, s: (b * per_b + s, g)),
        out_shape=jax.ShapeDtypeStruct((t, d), BF16),
        scratch_shapes=[pltpu.VMEM((gw, gw), F32), pltpu.VMEM((8, gw), F32)],
        compiler_params=_cparams(("parallel", "parallel", "arbitrary")),
        name="rwkv7",
    )(cols, cols, cols, cols, mu4, par, lw)


def _merge_kernel(x_ref, ya_ref, yb_ref, yc_ref, ga_ref, gb_ref, gc_ref, pa_ref, pb_ref, pc_ref,
                  wo_ref, gm_ref, gn_ref, sh_ref, sc_ref, wrh_ref, wrl_ref, br_ref,
                  xo_ref, h_ref, tw_ref, ti_ref, tp_ref, cnt_ref, run_ref):
    @pl.when(pl.program_id(0) == 0)
    def _():
        run_ref[...] = jnp.zeros_like(run_ref)

    m = _sigmoid(ga_ref[...].astype(F32)) * _dot(ya_ref[...], pa_ref[...])
    m = m + _sigmoid(gb_ref[...].astype(F32)) * _dot(yb_ref[...], pb_ref[...])
    m = m + _sigmoid(gc_ref[...].astype(F32)) * _dot(yc_ref[...], pc_ref[...])
    xn = x_ref[...] + gm_ref[0] * _dot(m.astype(BF16), wo_ref[...])
    xo_ref[...] = xn
    h = _rmsnorm(xn) * gn_ref[...] * (1.0 + sc_ref[0]) + sh_ref[0]
    h_ref[...] = h

    hh, hl = _split(h)
    logits = _dot(hh, wrh_ref[...]) + _dot(hh, wrl_ref[...]) + _dot(hl, wrh_ref[...]) + br_ref[...]
    lane = lax.broadcasted_iota(jnp.int32, logits.shape, 1)
    lanes = logits.shape[1]
    tw = jnp.zeros(logits.shape, F32)
    ti = jnp.zeros(logits.shape, jnp.int32)
    top = None
    wsum = None
    es, sel = [], []
    for kth in range(TOP_K):
        mk = jnp.max(logits, axis=1, keepdims=True)
        ik = jnp.min(jnp.where(logits == mk, lane, lanes), axis=1, keepdims=True)
        hit = lane == ik
        logits = jnp.where(hit, -jnp.inf, logits)
        if kth == 0:
            top = mk
        e = jnp.exp(mk - top)
        es.append(e)
        sel.append(hit)
        wsum = e if wsum is None else wsum + e
        ti = jnp.where(lane == kth, ik, ti)
    inv = 1.0 / wsum
    for kth in range(TOP_K):
        tw = jnp.where(lane == kth, es[kth] * inv, tw)
    tw_ref[...] = tw
    ti_ref[...] = ti

    multi = jnp.zeros(tw.shape, F32)
    for hit in sel:
        multi = multi + hit.astype(F32)
    rows = tw.shape[0]
    earlier = (lax.broadcasted_iota(jnp.int32, (rows, rows), 1)
               < lax.broadcasted_iota(jnp.int32, (rows, rows), 0)).astype(BF16)
    before = _dot(earlier, multi.astype(BF16)) + run_ref[...]
    tp = jnp.zeros(tw.shape, jnp.int32)
    for kth in range(TOP_K):
        pos = jnp.sum(jnp.where(sel[kth], before, 0.0), axis=1, keepdims=True)
        tp = jnp.where(lane == kth, pos.astype(jnp.int32), tp)
    tp_ref[...] = tp
    run_ref[...] += jnp.sum(multi, axis=0, keepdims=True)
    cnt_ref[...] = run_ref[...]


def _merge(xf, ya, yb, yc, cols, p_lru, p_sb, p_rwkv, w_out, g_mix, gain, shift, scale,
           w_router, b_router, seq):
    t, d = xf.shape
    tm = min(256, seq)
    per_b = seq // tm
    ne = w_router.shape[1]
    wr = jnp.zeros((d, 128), F32).at[:, :ne].set(w_router)
    wrh, wrl = _split(wr)
    br = jnp.full((1, 128), -1e30, F32).at[0, :ne].set(b_router)
    row = lambda i: (i, 0)
    const = lambda i: (0, 0)
    perb = lambda i: (i // per_b, 0, 0)
    return pl.pallas_call(
        _merge_kernel,
        grid=(t // tm,),
        in_specs=[pl.BlockSpec((tm, d), row), pl.BlockSpec((tm, d), row),
                  pl.BlockSpec((tm, d), row), pl.BlockSpec((tm, d), row),
                  pl.BlockSpec((tm, d), lambda i: (i, C_GA // d)),
                  pl.BlockSpec((tm, d), lambda i: (i, C_GB // d)),
                  pl.BlockSpec((tm, d), lambda i: (i, C_GC // d)),
                  pl.BlockSpec((d, d), const), pl.BlockSpec((d, d), const),
                  pl.BlockSpec((d, d), const), pl.BlockSpec((d, d), const),
                  pl.BlockSpec((1, 1, d), perb), pl.BlockSpec((1, d), const),
                  pl.BlockSpec((1, 1, d), perb), pl.BlockSpec((1, 1, d), perb),
                  pl.BlockSpec((d, 128), const), pl.BlockSpec((d, 128), const),
                  pl.BlockSpec((1, 128), const)],
        out_specs=[pl.BlockSpec((tm, d), row), pl.BlockSpec((tm, d), row),
                   pl.BlockSpec((tm, 128), row), pl.BlockSpec((tm, 128), row),
                   pl.BlockSpec((tm, 128), row), pl.BlockSpec((1, 128), const)],
        out_shape=[jax.ShapeDtypeStruct((t, d), F32), jax.ShapeDtypeStruct((t, d), F32),
                   jax.ShapeDtypeStruct((t, 128), F32), jax.ShapeDtypeStruct((t, 128), jnp.int32),
                   jax.ShapeDtypeStruct((t, 128), jnp.int32), jax.ShapeDtypeStruct((1, 128), F32)],
        scratch_shapes=[pltpu.VMEM((1, 128), F32)],
        compiler_params=_cparams(("arbitrary",)),
        name="merge_router",
    )(xf, ya, yb, yc, cols, cols, cols, p_lru.astype(BF16), p_sb.astype(BF16), p_rwkv.astype(BF16),
      w_out.astype(BF16), g_mix, gain.reshape(1, d), shift, scale, wrh, wrl, br)


def _row_copy(src, dst, sem, src_row, dst_row):
    return pltpu.make_async_copy(src.at[pl.ds(src_row, 1)], dst.at[pl.ds(dst_row, 1)], sem)


def _wait_rows(src, dst, sem, n_rows):
    def wait(i, carry):
        _row_copy(src, dst, sem, 0, 0).wait()
        return carry

    lax.fori_loop(0, n_rows, wait, 0, unroll=8)


def _dispatch_kernel(slot_ref, h_ref, xs_in, xs_out, sem):
    del xs_in
    n_rows = slot_ref.shape[2]

    def start(i, carry):
        for par in range(2):
            a = 2 * i + par
            tok = lax.shift_right_logical(a, TOP_K.bit_length() - 1)
            _row_copy(h_ref, xs_out, sem, tok, slot_ref[0, 0, a]).start(priority=par)
        return carry

    lax.fori_loop(0, n_rows // 2, start, 0, unroll=4)
    _wait_rows(h_ref, xs_out, sem, n_rows)


def _dispatch(hffn, tok_slots, n_slots):
    t, d = hffn.shape
    tt = min(DISPATCH_TOKENS, t)
    nt = t // tt
    return pl.pallas_call(
        _dispatch_kernel,
        grid=(nt,),
        in_specs=[pl.BlockSpec((1, 1, TOP_K * tt), lambda i: (i, 0, 0), memory_space=pltpu.SMEM),
                  pl.BlockSpec((tt, d), lambda i: (i, 0)),
                  pl.BlockSpec(memory_space=pl.ANY)],
        out_specs=pl.BlockSpec(memory_space=pl.ANY),
        out_shape=jax.ShapeDtypeStruct((n_slots, d), F32),
        scratch_shapes=[pltpu.SemaphoreType.DMA(())],
        input_output_aliases={2: 0},
        compiler_params=_cparams(("arbitrary",)),
        name="moe_dispatch",
    )(tok_slots.reshape(nt, 1, TOP_K * tt), hffn, jnp.zeros((n_slots, d), F32))


def _expert_kernel(be_ref, nused_ref, x_ref, wgu_ref, bgu_ref, wd_ref, bd_ref, o_ref,
                   wgu_bf, wd_bf):
    blk = pl.program_id(0)
    dff = wd_ref.shape[1]
    used = blk < nused_ref[0]
    fresh = jnp.logical_or(blk == 0, be_ref[blk] != be_ref[jnp.maximum(blk - 1, 0)])

    @pl.when(jnp.logical_and(used, fresh))
    def _():
        wgu_bf[...] = wgu_ref[0].astype(BF16)
        wd_bf[...] = wd_ref[0].astype(BF16)

    @pl.when(used)
    def _():
        gu = _dot(x_ref[...].astype(BF16), wgu_bf[...]) + bgu_ref[0]
        gate = jnp.minimum(gu[:, :dff], SWIGLU_LIMIT)
        up = jnp.clip(gu[:, dff:], -SWIGLU_LIMIT, SWIGLU_LIMIT)
        act = (up + 1.0) * gate * _sigmoid(SWIGLU_ALPHA * gate)
        o_ref[...] = _dot(act.astype(BF16), wd_bf[...]) + bd_ref[0]

    @pl.when(jnp.logical_not(used))
    def _():
        o_ref[...] = jnp.zeros_like(o_ref)


def _experts(xs, block_expert, n_used, w_gu, b_gu, w_down, b_down):
    d = xs.shape[1]
    n_blocks = block_expert.shape[0]
    ne, _, dff2 = w_gu.shape
    dff = dff2 // 2
    eb = EXPERT_BLOCK
    grid_spec = pltpu.PrefetchScalarGridSpec(
        num_scalar_prefetch=2,
        grid=(n_blocks,),
        in_specs=[pl.BlockSpec((eb, d), lambda b, be, nu: (b, 0)),
                  pl.BlockSpec((1, d, dff2), lambda b, be, nu: (be[b], 0, 0)),
                  pl.BlockSpec((1, 1, dff2), lambda b, be, nu: (be[b], 0, 0)),
                  pl.BlockSpec((1, dff, d), lambda b, be, nu: (be[b], 0, 0)),
                  pl.BlockSpec((1, 1, d), lambda b, be, nu: (be[b], 0, 0))],
        out_specs=pl.BlockSpec((eb, d), lambda b, be, nu: (b, 0)),
        scratch_shapes=[pltpu.VMEM((d, dff2), BF16), pltpu.VMEM((dff, d), BF16)],
    )
    return pl.pallas_call(
        _expert_kernel,
        grid_spec=grid_spec,
        out_shape=jax.ShapeDtypeStruct((n_blocks * eb, d), F32),
        compiler_params=_cparams(("arbitrary",)),
        name="moe_experts",
    )(block_expert, n_used, xs, w_gu, b_gu.reshape(ne, 1, dff2), w_down, b_down.reshape(ne, 1, d))


def _combine_kernel(slot_ref, slot_next_ref, yb_hbm, x_ref, tw_ref, gf_ref, gain_ref, o_ref,
                    buf, sem, *, final):
    tt = x_ref.shape[0]
    n_rows = TOP_K * tt
    i = pl.program_id(0)
    cur = i % 2

    def start_gather(idx_ref, slot):
        def start(r, carry):
            for par in range(2):
                a = 2 * r + par
                _row_copy(yb_hbm, buf.at[slot], sem.at[slot], idx_ref[0, 0, a], a).start(priority=par)
            return carry

        lax.fori_loop(0, n_rows // 2, start, 0, unroll=4)

    @pl.when(i == 0)
    def _():
        start_gather(slot_ref, 0)

    @pl.when(i + 1 < pl.num_programs(0))
    def _():
        start_gather(slot_next_ref, 1 - cur)

    _wait_rows(yb_hbm, buf.at[cur], sem.at[cur], n_rows)
    ffn = tw_ref[:, 0:1] * buf[cur, 0:tt, :]
    for kth in range(1, TOP_K):
        ffn = ffn + tw_ref[:, kth:kth + 1] * buf[cur, kth * tt:(kth + 1) * tt, :]
    xn = x_ref[...] + gf_ref[0] * ffn
    if final:
        xn = _rmsnorm(xn) * gain_ref[...]
    o_ref[...] = xn


def _combine(xf, ybuf, tok_slots, tw, g_ffn, final_gain, seq, final):
    t, d = xf.shape
    tt = min(COMBINE_TOKENS, seq)
    per_b = seq // tt
    nt = t // tt
    slots = tok_slots.reshape(nt, tt, TOP_K).transpose(0, 2, 1).reshape(nt, 1, TOP_K * tt)
    return pl.pallas_call(
        functools.partial(_combine_kernel, final=final),
        grid=(nt,),
        in_specs=[pl.BlockSpec((1, 1, TOP_K * tt), lambda i: (i, 0, 0), memory_space=pltpu.SMEM),
                  pl.BlockSpec((1, 1, TOP_K * tt), lambda i: (jnp.minimum(i + 1, nt - 1), 0, 0),
                               memory_space=pltpu.SMEM),
                  pl.BlockSpec(memory_space=pl.ANY),
                  pl.BlockSpec((tt, d), lambda i: (i, 0)),
                  pl.BlockSpec((tt, 128), lambda i: (i, 0)),
                  pl.BlockSpec((1, 1, d), lambda i: (i // per_b, 0, 0)),
                  pl.BlockSpec((1, d), lambda i: (0, 0))],
        out_specs=pl.BlockSpec((tt, d), lambda i: (i, 0)),
        out_shape=jax.ShapeDtypeStruct((t, d), F32),
        scratch_shapes=[pltpu.VMEM((2, TOP_K * tt, d), F32), pltpu.SemaphoreType.DMA((2,))],
        compiler_params=_cparams(("arbitrary",)),
        name="moe_combine",
    )(slots, slots, ybuf, xf, tw, g_ffn, final_gain.reshape(1, d))


def _route_plan(top_idx, top_pos, counts_f32, n_tok):
    counts = counts_f32[0, :N_EXPERTS].astype(jnp.int32)
    padded = (counts + EXPERT_BLOCK - 1) // EXPERT_BLOCK * EXPERT_BLOCK
    pad_end = jnp.cumsum(padded)
    pad_start = pad_end - padded
    n_blocks = -(-(n_tok * TOP_K) // EXPERT_BLOCK) + N_EXPERTS
    onehot = top_idx[:, :, None] == jnp.arange(N_EXPERTS, dtype=jnp.int32)
    tok_slots = top_pos + jnp.sum(jnp.where(onehot, pad_start, 0), axis=-1)
    block_start = jnp.arange(n_blocks, dtype=jnp.int32) * EXPERT_BLOCK
    block_expert = jnp.minimum(jnp.sum(pad_end[None, :] <= block_start[:, None], axis=1),
                               N_EXPERTS - 1).astype(jnp.int32)
    n_used = (pad_end[-1] // EXPERT_BLOCK).astype(jnp.int32).reshape(1)
    return tok_slots.astype(jnp.int32), block_expert, n_used, n_blocks * EXPERT_BLOCK


def _in_weight(w_in_l):
    lo0 = C_GA
    return jnp.concatenate([w_in_l[:, :lo0], w_in_l[:, lo0 + RW_LORA:], w_in_l[:, lo0:lo0 + RW_LORA]],
                           axis=1).astype(BF16)


def kernel(x, c, w_ada, b_ada, norm_mix, norm_moe, norm_final, w_in, conv_w, conv_b, lru_wa, lru_ba, lru_wx, lru_bx, lru_lambda, rw_mu, rw_w0, rw_w_up, rw_a0, rw_a_up, rw_g_up, rw_k_k, rw_k_a, rw_r_k, rw_lnx_w, rw_lnx_b, p_lru, p_sb, p_rwkv, w_out, w_router, b_router, w_gu, b_gu, w_down, b_down):
    bsz, seq, d = x.shape
    depth = w_in.shape[0]
    xf = x.reshape(bsz * seq, d)
    ada = _ada(c, w_ada, b_ada)
    for l in range(depth):
        sh_mix, sc_mix, g_mix, sh_ffn, sc_ffn, g_ffn = [
            ada[l, :, i * d:(i + 1) * d].reshape(bsz, 1, d) for i in range(N_ADA)]
        cols = _inproj(xf, norm_mix[l], sh_mix, sc_mix, _in_weight(w_in[l]), seq)
        ya = _rglru(cols, conv_w[l], conv_b[l], lru_wa[l], lru_ba[l], lru_wx[l], lru_bx[l],
                    lru_lambda[l], bsz, seq)
        yb = _sb_attention(cols, bsz, seq)
        yc = _rwkv(cols, rw_mu[l], rw_w0[l], rw_w_up[l], rw_a0[l], rw_a_up[l], rw_g_up[l],
                   rw_k_k[l], rw_k_a[l], rw_r_k[l], rw_lnx_w[l], rw_lnx_b[l], bsz, seq)
        xf, hffn, tw, ti, tp, cnt = _merge(xf, ya, yb, yc, cols, p_lru[l], p_sb[l], p_rwkv[l],
                                           w_out[l], g_mix, norm_moe[l], sh_ffn, sc_ffn,
                                           w_router[l], b_router[l], seq)
        tok_slots, block_expert, n_used, n_slots = _route_plan(
            ti[:, :TOP_K], tp[:, :TOP_K], cnt, bsz * seq)
        xs = _dispatch(hffn, tok_slots, n_slots)
        ybuf = _experts(xs, block_expert, n_used, w_gu[l], b_gu[l], w_down[l], b_down[l])
        xf = _combine(xf, ybuf, tok_slots, tw, g_ffn, norm_final, seq, final=(l == depth - 1))
    return xf.reshape(bsz, seq, d)
```

```python
import functools

import jax
import jax.numpy as jnp
from jax import lax
from jax.experimental import pallas as pl
from jax.experimental.pallas import tpu as pltpu

F32 = jnp.float32
BF16 = jnp.bfloat16

D_MODEL = 1024
NORM_EPS = 1e-6
N_ADA = 6

LRU_BLOCK_DIM = 64
CONV_WIDTH = 4
LRU_C = 8.0

SB_HEADS = 8
SB_HEAD_DIM = 128

RW_HEAD_DIM = 64
RW_LORA = 256
RW_GN_EPS = 1e-5 * RW_HEAD_DIM
RW_CHUNK = 64
RW_GROUP = 256
RW_TILE = 512

N_EXPERTS = 32
TOP_K = 4
SWIGLU_LIMIT = 7.0
SWIGLU_ALPHA = 1.702
EXPERT_BLOCK = 512
DISPATCH_TOKENS = 256
COMBINE_TOKENS = 128

C_LX, C_LG, C_Q, C_K, C_V, C_RR, C_RK, C_RV, C_GA, C_GB, C_GC, C_LO = (
    0, 1024, 2048, 3072, 4096, 5120, 6144, 7168, 8192, 9216, 10240, 11264)

VMEM_LIMIT_BYTES = 56 * 1024 * 1024
F32_EXP_ZERO = -104.0


def _cparams(sem):
    return pltpu.CompilerParams(dimension_semantics=sem, vmem_limit_bytes=VMEM_LIMIT_BYTES)


def _dot(a, b):
    return jnp.dot(a, b, preferred_element_type=F32)


def _dot_nt(a, b):
    return lax.dot_general(a, b, (((1,), (1,)), ((), ())), preferred_element_type=F32)


def _split(x):
    hi = x.astype(BF16)
    lo = (x - hi.astype(F32)).astype(BF16)
    return hi, lo


def _softplus(x):
    return jnp.maximum(x, 0.0) + jnp.log(1.0 + jnp.exp(-jnp.abs(x)))


def _sigmoid(x):
    return 1.0 / (1.0 + jnp.exp(-x))


def _rmsnorm(x):
    return x * lax.rsqrt(jnp.mean(x * x, axis=-1, keepdims=True) + NORM_EPS)


def _ada_kernel(c_ref, w_ref, b_ref, o_ref):
    c = c_ref[...]
    ca = c * _sigmoid(c)
    o_ref[0] = jnp.dot(ca, w_ref[0], preferred_element_type=F32,
                       precision=lax.Precision.HIGHEST) + b_ref[0]


def _ada(c, w_ada, b_ada):
    depth, d, n = w_ada.shape
    bsz = c.shape[0]
    c8 = jnp.zeros((8, d), F32).at[:bsz].set(c)
    out = pl.pallas_call(
        _ada_kernel,
        grid=(depth, n // d),
        in_specs=[pl.BlockSpec((8, d), lambda l, j: (0, 0)),
                  pl.BlockSpec((1, d, d), lambda l, j: (l, 0, j)),
                  pl.BlockSpec((1, 1, d), lambda l, j: (l, 0, j))],
        out_specs=pl.BlockSpec((1, 8, d), lambda l, j: (l, 0, j)),
        out_shape=jax.ShapeDtypeStruct((depth, 8, n), F32),
        compiler_params=_cparams(("parallel", "parallel")),
        name="ada",
    )(c8, w_ada, b_ada.reshape(depth, 1, n))
    return out[:, :bsz]


def _inproj_kernel(x_ref, g_ref, sh_ref, sc_ref, w_ref, o_ref, h_ref):
    @pl.when(pl.program_id(1) == 0)
    def _():
        h = _rmsnorm(x_ref[...]) * g_ref[...] * (1.0 + sc_ref[0]) + sh_ref[0]
        h_ref[...] = h.astype(BF16)

    o_ref[...] = _dot(h_ref[...], w_ref[...]).astype(BF16)


def _inproj(xf, gain, shift, scale, w_bf16, seq):
    t, d = xf.shape
    n = w_bf16.shape[1]
    tm = min(1024, seq)
    tn = 1280
    per_b = seq // tm
    return pl.pallas_call(
        _inproj_kernel,
        grid=(t // tm, n // tn),
        in_specs=[pl.BlockSpec((tm, d), lambda i, j: (i, 0)),
                  pl.BlockSpec((1, d), lambda i, j: (0, 0)),
                  pl.BlockSpec((1, 1, d), lambda i, j: (i // per_b, 0, 0)),
                  pl.BlockSpec((1, 1, d), lambda i, j: (i // per_b, 0, 0)),
                  pl.BlockSpec((d, tn), lambda i, j: (0, j))],
        out_specs=pl.BlockSpec((tm, tn), lambda i, j: (i, j)),
        out_shape=jax.ShapeDtypeStruct((t, n), BF16),
        scratch_shapes=[pltpu.VMEM((tm, d), BF16)],
        compiler_params=_cparams(("parallel", "arbitrary")),
        name="inproj",
    )(xf, gain.reshape(1, d), shift, scale, w_bf16)


def _rglru_kernel(x_ref, gate_ref, cw_ref, cb_ref, wax_ref, bax_ref, lam_ref, o_ref,
                  tail_ref, h_ref):
    ts, d = x_ref.shape

    @pl.when(pl.program_id(1) == 0)
    def _():
        tail_ref[...] = jnp.zeros_like(tail_ref)
        h_ref[...] = jnp.zeros_like(h_ref)

    x = x_ref[...].astype(F32)
    tail = tail_ref[...]
    row8 = lax.broadcasted_iota(jnp.int32, (8, d), 0)
    xc = cb_ref[...] + cw_ref[0:1, :] * x
    for k in range(1, CONV_WIDTH):
        xs = pltpu.roll(x, k, 0)
        head = jnp.where(row8 < k, pltpu.roll(tail, k, 0), xs[0:8])
        xs = jnp.concatenate([head, xs[8:]], axis=0)
        xc = xc + cw_ref[k:k + 1, :] * xs
    tail_ref[...] = x[ts - 8:]

    rs, is_ = [], []
    for blk in range(d // 128):
        xb = xc[:, blk * 128:(blk + 1) * 128].astype(BF16)
        ri = _dot(xb, wax_ref[blk]) + bax_ref[blk]
        rs.append(ri[:, :128])
        is_.append(ri[:, 128:])
    r = _sigmoid(jnp.concatenate(rs, axis=1))
    i = _sigmoid(jnp.concatenate(is_, axis=1))
    log_a = (-LRU_C) * r * _softplus(-lam_ref[...])
    a = jnp.exp(log_a)
    b = jnp.sqrt(1.0 - jnp.exp(2.0 * log_a)) * (i * xc)

    row = lax.broadcasted_iota(jnp.int32, (ts, d), 0)
    sh = 1
    while sh < ts:
        a_s = pltpu.roll(a, sh, 0)
        b_s = pltpu.roll(b, sh, 0)
        m = row >= sh
        b = jnp.where(m, a * b_s + b, b)
        a = jnp.where(m, a * a_s, a)
        sh *= 2
    h = a * h_ref[...] + b
    h_ref[...] = h[ts - 1:]

    g = gate_ref[...].astype(F32)
    gelu = 0.5 * g * (1.0 + jnp.tanh(0.7978845608028654 * (g + 0.044715 * g * g * g)))
    o_ref[...] = (gelu * h).astype(BF16)


def _rglru(cols, conv_w, conv_b, wa, ba, wx, bx, lam, bsz, seq):
    t = cols.shape[0]
    d = D_MODEL
    ts = min(256, seq)
    per_b = seq // ts
    nb = d // 128

    def bdiag(w):
        w = w.reshape(nb, 2, LRU_BLOCK_DIM, LRU_BLOCK_DIM)
        z = jnp.zeros((nb, LRU_BLOCK_DIM, LRU_BLOCK_DIM), w.dtype)
        top = jnp.concatenate([w[:, 0], z], axis=2)
        bot = jnp.concatenate([z, w[:, 1]], axis=2)
        return jnp.concatenate([top, bot], axis=1)

    wax = jnp.concatenate([bdiag(wa), bdiag(wx)], axis=2).astype(BF16)
    bax = jnp.concatenate([ba.reshape(nb, 1, 128), bx.reshape(nb, 1, 128)], axis=2)
    return pl.pallas_call(
        _rglru_kernel,
        grid=(bsz, per_b),
        in_specs=[pl.BlockSpec((ts, d), lambda b, s: (b * per_b + s, C_LX // d)),
                  pl.BlockSpec((ts, d), lambda b, s: (b * per_b + s, C_LG // d)),
                  pl.BlockSpec((CONV_WIDTH, d), lambda b, s: (0, 0)),
                  pl.BlockSpec((1, d), lambda b, s: (0, 0)),
                  pl.BlockSpec((nb, 128, 256), lambda b, s: (0, 0, 0)),
                  pl.BlockSpec((nb, 1, 256), lambda b, s: (0, 0, 0)),
                  pl.BlockSpec((1, d), lambda b, s: (0, 0))],
        out_specs=pl.BlockSpec((ts, d), lambda b, s: (b * per_b + s, 0)),
        out_shape=jax.ShapeDtypeStruct((t, d), BF16),
        scratch_shapes=[pltpu.VMEM((8, d), F32), pltpu.VMEM((1, d), F32)],
        compiler_params=_cparams(("parallel", "arbitrary")),
        name="rglru",
    )(cols, cols, conv_w, conv_b.reshape(1, d), wax, bax, lam.reshape(1, d))


def _sb_kernel(q_ref, k_ref, v_ref, o_ref, acc_ref, carry_ref):
    tq = q_ref.shape[0]
    dh = SB_HEAD_DIM
    nh = q_ref.shape[1] // dh
    heads = range(nh)
    qi = pl.program_id(2)
    scale = dh ** -0.5
    acc_ref[...] = jnp.zeros_like(acc_ref)
    carry_ref[...] = jnp.zeros_like(carry_ref)
    rowi = lax.broadcasted_iota(jnp.int32, (tq, tq), 0)
    coli = lax.broadcasted_iota(jnp.int32, (tq, tq), 1)
    upper = (rowi > coli).astype(BF16)

    def block(j, masked):
        rows = pl.ds(pl.multiple_of(j * tq, tq), tq)
        lanes = [slice(h * dh, (h + 1) * dh) for h in heads]
        valid = coli < rowi
        z = [_dot_nt(q_ref[:, lanes[h]], k_ref[rows, lanes[h]]) * scale for h in heads]
        sp = [_softplus(z[h]) for h in heads]
        if masked:
            log_stay = [jnp.where(valid, -sp[h], 0.0) for h in heads]
        else:
            log_stay = [-sp[h] for h in heads]
        parts = [_split(log_stay[h]) for h in heads]
        carry = [carry_ref[h] for h in heads]
        later = [_dot(parts[h][0], upper) + _dot(parts[h][1], upper) + carry[h] for h in heads]
        attn = [jnp.exp(z[h] - sp[h] + later[h]) for h in heads]
        if masked:
            attn = [jnp.where(valid, attn[h], 0.0) for h in heads]
        for h in heads:
            acc_ref[:, lanes[h]] += _dot(attn[h].astype(BF16), v_ref[rows, lanes[h]])
        carry = [carry[h] + jnp.sum(log_stay[h], axis=1, keepdims=True) for h in heads]
        top = carry[0]
        for h in heads:
            carry_ref[h] = carry[h]
            top = jnp.maximum(top, carry[h])
        return jnp.max(top) >= F32_EXP_ZERO

    def body(state):
        j, _ = state
        return j - 1, block(j, False)

    def cond(state):
        j, go = state
        return jnp.logical_and(j >= 0, go)

    lax.while_loop(cond, body, (qi - 1, block(qi, True)))
    o_ref[...] = acc_ref[...].astype(BF16)


def _sb_attention(cols, bsz, seq):
    t = cols.shape[0]
    dh = SB_HEAD_DIM
    hg = 4
    gw = hg * dh
    tq = min(128, seq)
    nq = seq // tq
    qb, kb, vb = C_Q // gw, C_K // gw, C_V // gw
    return pl.pallas_call(
        _sb_kernel,
        grid=(bsz, SB_HEADS // hg, nq),
        in_specs=[pl.BlockSpec((tq, gw), lambda b, g, i: (b * nq + i, qb + g)),
                  pl.BlockSpec((seq, gw), lambda b, g, i: (b, kb + g)),
                  pl.BlockSpec((seq, gw), lambda b, g, i: (b, vb + g))],
        out_specs=pl.BlockSpec((tq, gw), lambda b, g, i: (b * nq + i, g)),
        out_shape=jax.ShapeDtypeStruct((t, SB_HEADS * dh), BF16),
        scratch_shapes=[pltpu.VMEM((tq, gw), F32), pltpu.VMEM((hg, tq, 1), F32)],
        compiler_params=_cparams(("parallel", "parallel", "arbitrary")),
        name="sb_attention",
    )(cols, cols, cols)


def _rwkv_kernel(r_ref, k_ref, v_ref, l_ref, mu_ref, par_ref, lw_ref, o_ref, st_ref, prev_ref):
    ts, gw = r_ref.shape
    c = RW_CHUNK
    nch = ts // c
    n = RW_HEAD_DIM
    nh = gw // n

    @pl.when(pl.program_id(2) == 0)
    def _():
        st_ref[...] = jnp.zeros_like(st_ref)
        prev_ref[...] = jnp.zeros_like(prev_ref)

    row = lax.broadcasted_iota(jnp.int32, (ts, gw), 0)

    def shift_mix(ref, idx):
        x = ref[...].astype(F32)
        prev = jnp.where(row == 0, prev_ref[idx:idx + 1, :], pltpu.roll(x, 1, 0))
        prev_ref[idx:idx + 1, :] = x[ts - 1:]
        return x + mu_ref[idx:idx + 1, :] * (prev - x)

    r = shift_mix(r_ref, 0)
    k = shift_mix(k_ref, 1)
    v = shift_mix(v_ref, 2)
    lo = shift_mix(l_ref, 3)

    w0, a0 = par_ref[0:1, :], par_ref[1:2, :]
    k_k, k_a, r_k = par_ref[2:3, :], par_ref[3:4, :], par_ref[4:5, :]
    lnx_w, lnx_b = par_ref[5:6, :], par_ref[6:7, :]

    dw = _dot(jnp.tanh(lo).astype(BF16), lw_ref[0])
    da = _dot(lo.astype(BF16), lw_ref[1])
    g = _dot(_sigmoid(lo).astype(BF16), lw_ref[2])
    w_log = -_softplus(-(w0 + dw)) - 0.5
    ld = -jnp.exp(w_log)
    rate = _sigmoid(a0 + da)

    ri = lax.broadcasted_iota(jnp.int32, (gw, gw), 0)
    ci = lax.broadcasted_iota(jnp.int32, (gw, gw), 1)
    shift_n = n.bit_length() - 1
    same_blk = (ri >> shift_n) == (ci >> shift_n)
    seg = same_blk.astype(BF16)
    tri = (lax.broadcasted_iota(jnp.int32, (c, c), 1)
           <= lax.broadcasted_iota(jnp.int32, (c, c), 0)).astype(BF16)

    def segsum(x):
        hi, lo_ = _split(x)
        return _dot(hi, seg) + _dot(lo_, seg)

    kk = k * k_k
    kk = kk * lax.rsqrt(jnp.maximum(segsum(kk * kk), 1e-24))
    k2 = k * (1.0 + (rate - 1.0) * k_a)
    av = -kk
    bv = kk * rate

    ld_hi, ld_lo = _split(ld)

    t_i = lax.broadcasted_iota(jnp.int32, (c, gw), 0)
    s_i = lax.broadcasted_iota(jnp.int32, (c, gw), 1) & (c - 1)
    strict = s_i < t_i
    incl = s_i <= t_i
    eye = (s_i == t_i).astype(F32)

    def stacked(x):
        xb = x.astype(BF16)
        return jnp.where(same_blk, jnp.concatenate([xb] * nh, axis=0), jnp.zeros((), BF16))

    chs = range(nch)
    sls = [slice(ch * c, (ch + 1) * c) for ch in chs]
    cl_c = [_dot(tri, ld_hi[sl]) + _dot(tri, ld_lo[sl]) for sl in sls]
    cl_end = [x[c - 1:c] for x in cl_c]
    e_pos = [jnp.exp(x) for x in cl_c]
    e_neg = [jnp.exp(-x) for x in cl_c]
    e_end = [jnp.exp(cl_end[ch] - cl_c[ch]) for ch in chs]
    rt = [r[sls[ch]] * e_pos[ch] for ch in chs]
    at = [av[sls[ch]] * jnp.exp(cl_c[ch] - ld[sls[ch]]) for ch in chs]
    bt = [bv[sls[ch]] * e_neg[ch] for ch in chs]
    kt = [k2[sls[ch]] * e_neg[ch] for ch in chs]
    bh = [bv[sls[ch]] * e_end[ch] for ch in chs]
    kh = [k2[sls[ch]] * e_end[ch] for ch in chs]
    vc = [v[sl] for sl in sls]
    gam = [jnp.exp(x) for x in cl_end]

    ar = [jnp.concatenate([at[ch], rt[ch]], axis=0).astype(BF16) for ch in chs]
    ab = [_dot_nt(ar[ch], stacked(bt[ch])) for ch in chs]
    ak = [_dot_nt(ar[ch], stacked(kt[ch])) for ch in chs]
    a_m = [jnp.where(strict, x[:c], 0.0) for x in ab]
    pb = [jnp.where(incl, x[c:], 0.0).astype(BF16) for x in ab]
    ak_m = [jnp.where(strict, x[:c], 0.0).astype(BF16) for x in ak]
    pk = [jnp.where(incl, x[c:], 0.0).astype(BF16) for x in ak]

    tinv = [eye + x for x in a_m]
    apow = a_m
    p = 2
    while p < c:
        apow = [_dot(x.astype(BF16), stacked(x)) for x in apow]
        tinv = [tinv[ch] + _dot(tinv[ch].astype(BF16), stacked(apow[ch])) for ch in chs]
        p *= 2
    tb = [x.astype(BF16) for x in tinv]

    vs = [stacked(x) for x in vc]
    w = [_dot(tb[ch], stacked(at[ch])) for ch in chs]
    akv = [_dot(ak_m[ch], vs[ch]) for ch in chs]
    z = [_dot(tb[ch], stacked(akv[ch])) for ch in chs]
    rhat = [(rt[ch] + _dot(pb[ch], stacked(w[ch]))).astype(BF16) for ch in chs]
    yhat = [_dot(pb[ch], stacked(z[ch])) + _dot(pk[ch], vs[ch]) for ch in chs]
    zero = jnp.zeros((c, gw), F32)
    lt = [jnp.concatenate([w[ch], z[ch], vc[ch], zero], axis=0).T.astype(BF16) for ch in chs]
    x1 = [jnp.where(same_blk, _dot(lt[ch], jnp.concatenate([bh[ch], zero, zero, zero], 0).astype(BF16)),
                    0.0).astype(BF16) for ch in chs]
    x2 = [jnp.where(same_blk, _dot(lt[ch], jnp.concatenate([zero, bh[ch], kh[ch], zero], 0).astype(BF16)),
                    0.0) for ch in chs]

    st = st_ref[...]
    ys = []
    for ch in chs:
        stb = st.astype(BF16)
        ys.append(_dot_nt(rhat[ch], stb) + yhat[ch])
        st = gam[ch] * st + _dot(stb, x1[ch]) + x2[ch]
    st_ref[...] = st
    y = jnp.concatenate(ys, axis=0)

    inv_n = 1.0 / n
    mean = segsum(y) * inv_n
    dev = y - mean
    var = segsum(dev * dev) * inv_n
    yn = dev * lax.rsqrt(var + RW_GN_EPS) * lnx_w + lnx_b
    bonus = segsum(r * k2 * r_k) * v
    o_ref[...] = ((yn + bonus) * g).astype(BF16)


def _rwkv(cols, mu, w0, w_up, a0, a_up, g_up, k_k, k_a, r_k, lnx_w, lnx_b, bsz, seq):
    t = cols.shape[0]
    d = D_MODEL
    gw = RW_GROUP
    ts = min(RW_TILE, seq)
    per_b = seq // ts
    ng = d // gw
    mu4 = jnp.stack([mu[0:d], mu[d:2 * d], mu[2 * d:3 * d],
                     jnp.tile(mu[3 * d:], d // RW_LORA)], axis=0)
    par = jnp.stack([w0, a0, k_k, k_a, r_k.reshape(d), lnx_w, lnx_b, jnp.zeros((d,), F32)], axis=0)
    lw = jnp.zeros((3, RW_LORA, d), F32)
    lw = lw.at[0, 0:64].set(w_up).at[1, 64:128].set(a_up).at[2, 128:256].set(g_up).astype(BF16)
    rb, kb, vb, lb = C_RR // gw, C_RK // gw, C_RV // gw, C_LO // gw
    return pl.pallas_call(
        _rwkv_kernel,
        grid=(bsz, ng, per_b),
        in_specs=[pl.BlockSpec((ts, gw), lambda b, g, s: (b * per_b + s, rb + g)),
                  pl.BlockSpec((ts, gw), lambda b, g, s: (b * per_b + s, kb + g)),
                  pl.BlockSpec((ts, gw), lambda b, g, s: (b * per_b + s, vb + g)),
                  pl.BlockSpec((ts, gw), lambda b, g, s: (b * per_b + s, lb)),
                  pl.BlockSpec((4, gw), lambda b, g, s: (0, g)),
                  pl.BlockSpec((8, gw), lambda b, g, s: (0, g)),
                  pl.BlockSpec((3, RW_LORA, gw), lambda b, g, s: (0, 0, g))],
        out_specs=pl.BlockSpec((ts, gw), lambda b, g, s: (b * per_b + s, g)),
        out_shape=jax.ShapeDtypeStruct((t, d), BF16),
        scratch_shapes=[pltpu.VMEM((gw, gw), F32), pltpu.VMEM((8, gw), F32)],
        compiler_params=_cparams(("parallel", "parallel", "arbitrary")),
        name="rwkv7",
    )(cols, cols, cols, cols, mu4, par, lw)


def _merge_kernel(x_ref, ya_ref, yb_ref, yc_ref, ga_ref, gb_ref, gc_ref, pa_ref, pb_ref, pc_ref,
                  wo_ref, gm_ref, gn_ref, sh_ref, sc_ref, wrh_ref, wrl_ref, br_ref,
                  xo_ref, h_ref, tw_ref, ti_ref, tp_ref, cnt_ref, run_ref):
    @pl.when(pl.program_id(0) == 0)
    def _():
        run_ref[...] = jnp.zeros_like(run_ref)

    m = _sigmoid(ga_ref[...].astype(F32)) * _dot(ya_ref[...], pa_ref[...])
    m = m + _sigmoid(gb_ref[...].astype(F32)) * _dot(yb_ref[...], pb_ref[...])
    m = m + _sigmoid(gc_ref[...].astype(F32)) * _dot(yc_ref[...], pc_ref[...])
    xn = x_ref[...] + gm_ref[0] * _dot(m.astype(BF16), wo_ref[...])
    xo_ref[...] = xn
    h = _rmsnorm(xn) * gn_ref[...] * (1.0 + sc_ref[0]) + sh_ref[0]
    h_ref[...] = h

    hh, hl = _split(h)
    logits = _dot(hh, wrh_ref[...]) + _dot(hh, wrl_ref[...]) + _dot(hl, wrh_ref[...]) + br_ref[...]
    lane = lax.broadcasted_iota(jnp.int32, logits.shape, 1)
    lanes = logits.shape[1]
    tw = jnp.zeros(logits.shape, F32)
    ti = jnp.zeros(logits.shape, jnp.int32)
    top = None
    wsum = None
    es, sel = [], []
    for kth in range(TOP_K):
        mk = jnp.max(logits, axis=1, keepdims=True)
        ik = jnp.min(jnp.where(logits == mk, lane, lanes), axis=1, keepdims=True)
        hit = lane == ik
        logits = jnp.where(hit, -jnp.inf, logits)
        if kth == 0:
            top = mk
        e = jnp.exp(mk - top)
        es.append(e)
        sel.append(hit)
        wsum = e if wsum is None else wsum + e
        ti = jnp.where(lane == kth, ik, ti)
    inv = 1.0 / wsum
    for kth in range(TOP_K):
        tw = jnp.where(lane == kth, es[kth] * inv, tw)
    tw_ref[...] = tw
    ti_ref[...] = ti

    multi = jnp.zeros(tw.shape, F32)
    for hit in sel:
        multi = multi + hit.astype(F32)
    rows = tw.shape[0]
    earlier = (lax.broadcasted_iota(jnp.int32, (rows, rows), 1)
               < lax.broadcasted_iota(jnp.int32, (rows, rows), 0)).astype(BF16)
    before = _dot(earlier, multi.astype(BF16)) + run_ref[...]
    tp = jnp.zeros(tw.shape, jnp.int32)
    for kth in range(TOP_K):
        pos = jnp.sum(jnp.where(sel[kth], before, 0.0), axis=1, keepdims=True)
        tp = jnp.where(lane == kth, pos.astype(jnp.int32), tp)
    tp_ref[...] = tp
    run_ref[...] += jnp.sum(multi, axis=0, keepdims=True)
    cnt_ref[...] = run_ref[...]


def _merge(xf, ya, yb, yc, cols, p_lru, p_sb, p_rwkv, w_out, g_mix, gain, shift, scale,
           w_router, b_router, seq):
    t, d = xf.shape
    tm = min(256, seq)
    per_b = seq // tm
    ne = w_router.shape[1]
    wr = jnp.zeros((d, 128), F32).at[:, :ne].set(w_router)
    wrh, wrl = _split(wr)
    br = jnp.full((1, 128), -1e30, F32).at[0, :ne].set(b_router)
    row = lambda i: (i, 0)
    const = lambda i: (0, 0)
    perb = lambda i: (i // per_b, 0, 0)
    return pl.pallas_call(
        _merge_kernel,
        grid=(t // tm,),
        in_specs=[pl.BlockSpec((tm, d), row), pl.BlockSpec((tm, d), row),
                  pl.BlockSpec((tm, d), row), pl.BlockSpec((tm, d), row),
                  pl.BlockSpec((tm, d), lambda i: (i, C_GA // d)),
                  pl.BlockSpec((tm, d), lambda i: (i, C_GB // d)),
                  pl.BlockSpec((tm, d), lambda i: (i, C_GC // d)),
                  pl.BlockSpec((d, d), const), pl.BlockSpec((d, d), const),
                  pl.BlockSpec((d, d), const), pl.BlockSpec((d, d), const),
                  pl.BlockSpec((1, 1, d), perb), pl.BlockSpec((1, d), const),
                  pl.BlockSpec((1, 1, d), perb), pl.BlockSpec((1, 1, d), perb),
                  pl.BlockSpec((d, 128), const), pl.BlockSpec((d, 128), const),
                  pl.BlockSpec((1, 128), const)],
        out_specs=[pl.BlockSpec((tm, d), row), pl.BlockSpec((tm, d), row),
                   pl.BlockSpec((tm, 128), row), pl.BlockSpec((tm, 128), row),
                   pl.BlockSpec((tm, 128), row), pl.BlockSpec((1, 128), const)],
        out_shape=[jax.ShapeDtypeStruct((t, d), F32), jax.ShapeDtypeStruct((t, d), F32),
                   jax.ShapeDtypeStruct((t, 128), F32), jax.ShapeDtypeStruct((t, 128), jnp.int32),
                   jax.ShapeDtypeStruct((t, 128), jnp.int32), jax.ShapeDtypeStruct((1, 128), F32)],
        scratch_shapes=[pltpu.VMEM((1, 128), F32)],
        compiler_params=_cparams(("arbitrary",)),
        name="merge_router",
    )(xf, ya, yb, yc, cols, cols, cols, p_lru.astype(BF16), p_sb.astype(BF16), p_rwkv.astype(BF16),
      w_out.astype(BF16), g_mix, gain.reshape(1, d), shift, scale, wrh, wrl, br)


def _row_copy(src, dst, sem, src_row, dst_row):
    return pltpu.make_async_copy(src.at[pl.ds(src_row, 1)], dst.at[pl.ds(dst_row, 1)], sem)


def _wait_rows(src, dst, sem, n_rows):
    def wait(i, carry):
        _row_copy(src, dst, sem, 0, 0).wait()
        return carry

    lax.fori_loop(0, n_rows, wait, 0, unroll=8)


def _dispatch_kernel(slot_ref, free_ref, h_ref, xs_out, zrow, sem):
    n_tok = h_ref.shape[0]
    n_free = free_ref.shape[2]

    @pl.when(pl.program_id(0) == 0)
    def _():
        zrow[...] = jnp.zeros_like(zrow)

    def start(g, carry):
        base = pl.multiple_of(g * 8, 8)
        for s in range(8):
            for kth in range(TOP_K):
                a = s * TOP_K + kth
                slot = slot_ref[0, 0, g * (8 * TOP_K) + a]
                _row_copy(h_ref, xs_out, sem, base + s, slot).start(priority=a % 2)
        return carry

    lax.fori_loop(0, n_tok // 8, start, 0)

    def start_free(g, carry):
        for s in range(8):
            _row_copy(zrow, xs_out, sem, 0, free_ref[0, 0, g * 8 + s]).start(priority=s % 2)
        return carry

    lax.fori_loop(0, n_free // 8, start_free, 0)
    _wait_rows(h_ref, xs_out, sem, n_tok * TOP_K + n_free)


def _dispatch(hffn, tok_slots, free_slots, n_slots):
    t, d = hffn.shape
    tt = min(DISPATCH_TOKENS, t)
    nt = t // tt
    n_free = free_slots.shape[0] // nt
    return pl.pallas_call(
        _dispatch_kernel,
        grid=(nt,),
        in_specs=[pl.BlockSpec((1, 1, TOP_K * tt), lambda i: (i, 0, 0), memory_space=pltpu.SMEM),
                  pl.BlockSpec((1, 1, n_free), lambda i: (i, 0, 0), memory_space=pltpu.SMEM),
                  pl.BlockSpec((tt, d), lambda i: (i, 0))],
        out_specs=pl.BlockSpec(memory_space=pl.ANY),
        out_shape=jax.ShapeDtypeStruct((n_slots, d), F32),
        scratch_shapes=[pltpu.VMEM((8, d), F32), pltpu.SemaphoreType.DMA(())],
        compiler_params=_cparams(("arbitrary",)),
        name="moe_dispatch",
    )(tok_slots.reshape(nt, 1, TOP_K * tt), free_slots.reshape(nt, 1, n_free), hffn)


def _expert_kernel(be_ref, nused_ref, x_ref, wgu_ref, bgu_ref, wd_ref, bd_ref, o_ref,
                   wgu_bf, wd_bf):
    blk = pl.program_id(0)
    dff = wd_ref.shape[2]
    used = blk < nused_ref[0]
    fresh = jnp.logical_or(blk == 0, be_ref[blk] != be_ref[jnp.maximum(blk - 1, 0)])

    @pl.when(jnp.logical_and(used, fresh))
    def _():
        wgu_bf[...] = wgu_ref[0, 0].astype(BF16)
        wd_bf[...] = wd_ref[0, 0].astype(BF16)

    @pl.when(used)
    def _():
        gu = _dot(x_ref[...].astype(BF16), wgu_bf[...]) + bgu_ref[0, 0]
        gate = jnp.minimum(gu[:, :dff], SWIGLU_LIMIT)
        up = jnp.clip(gu[:, dff:], -SWIGLU_LIMIT, SWIGLU_LIMIT)
        act = (up + 1.0) * gate * _sigmoid(SWIGLU_ALPHA * gate)
        o_ref[...] = _dot(act.astype(BF16), wd_bf[...]) + bd_ref[0, 0]

    @pl.when(jnp.logical_not(used))
    def _():
        o_ref[...] = jnp.zeros_like(o_ref)


def _experts(xs, block_expert, n_used, w_gu, b_gu, w_down, b_down, layer):
    d = xs.shape[1]
    n_blocks = block_expert.shape[0]
    depth, ne, _, dff2 = w_gu.shape
    dff = dff2 // 2
    eb = EXPERT_BLOCK
    grid_spec = pltpu.PrefetchScalarGridSpec(
        num_scalar_prefetch=2,
        grid=(n_blocks,),
        in_specs=[pl.BlockSpec((eb, d), lambda b, be, nu: (jnp.minimum(b, nu[0] - 1), 0)),
                  pl.BlockSpec((1, 1, d, dff2), lambda b, be, nu: (layer, be[b], 0, 0)),
                  pl.BlockSpec((1, 1, 1, dff2), lambda b, be, nu: (layer, be[b], 0, 0)),
                  pl.BlockSpec((1, 1, dff, d), lambda b, be, nu: (layer, be[b], 0, 0)),
                  pl.BlockSpec((1, 1, 1, d), lambda b, be, nu: (layer, be[b], 0, 0))],
        out_specs=pl.BlockSpec((eb, d), lambda b, be, nu: (b, 0)),
        scratch_shapes=[pltpu.VMEM((d, dff2), BF16), pltpu.VMEM((dff, d), BF16)],
    )
    return pl.pallas_call(
        _expert_kernel,
        grid_spec=grid_spec,
        out_shape=jax.ShapeDtypeStruct((n_blocks * eb, d), F32),
        compiler_params=_cparams(("arbitrary",)),
        name="moe_experts",
    )(block_expert, n_used, xs, w_gu, b_gu.reshape(depth, ne, 1, dff2), w_down,
      b_down.reshape(depth, ne, 1, d))


def _combine_kernel(slot_ref, slot_next_ref, yb_hbm, x_ref, tw_ref, gf_ref, gain_ref, o_ref,
                    buf, sem, *, final):
    tt = x_ref.shape[0]
    n_rows = TOP_K * tt
    i = pl.program_id(0)
    cur = i % 2

    groups = tt // 8

    def start_gather(idx_ref, slot):
        def start(g, carry):
            for kth in range(TOP_K):
                dst = buf.at[slot, kth * groups + g]
                for s in range(8):
                    _row_copy(yb_hbm, dst, sem.at[slot], idx_ref[0, 0, kth * tt + g * 8 + s],
                              s).start(priority=s % 2)
            return carry

        lax.fori_loop(0, groups, start, 0)

    @pl.when(i == 0)
    def _():
        start_gather(slot_ref, 0)

    @pl.when(i + 1 < pl.num_programs(0))
    def _():
        start_gather(slot_next_ref, 1 - cur)

    _wait_rows(yb_hbm, buf.at[cur, 0], sem.at[cur], n_rows)
    d = x_ref.shape[1]
    ffn = tw_ref[:, 0:1] * buf[cur, 0:groups].reshape(tt, d)
    for kth in range(1, TOP_K):
        ffn = ffn + tw_ref[:, kth:kth + 1] * buf[cur, kth * groups:(kth + 1) * groups].reshape(tt, d)
    xn = x_ref[...] + gf_ref[0] * ffn
    if final:
        xn = _rmsnorm(xn) * gain_ref[...]
    o_ref[...] = xn


def _combine(xf, ybuf, tok_slots, tw, g_ffn, final_gain, seq, final):
    t, d = xf.shape
    tt = min(COMBINE_TOKENS, seq)
    per_b = seq // tt
    nt = t // tt
    slots = tok_slots.reshape(nt, tt, TOP_K).transpose(0, 2, 1).reshape(nt, 1, TOP_K * tt)
    return pl.pallas_call(
        functools.partial(_combine_kernel, final=final),
        grid=(nt,),
        in_specs=[pl.BlockSpec((1, 1, TOP_K * tt), lambda i: (i, 0, 0), memory_space=pltpu.SMEM),
                  pl.BlockSpec((1, 1, TOP_K * tt), lambda i: (jnp.minimum(i + 1, nt - 1), 0, 0),
                               memory_space=pltpu.SMEM),
                  pl.BlockSpec(memory_space=pl.ANY),
                  pl.BlockSpec((tt, d), lambda i: (i, 0)),
                  pl.BlockSpec((tt, 128), lambda i: (i, 0)),
                  pl.BlockSpec((1, 1, d), lambda i: (i // per_b, 0, 0)),
                  pl.BlockSpec((1, d), lambda i: (0, 0))],
        out_specs=pl.BlockSpec((tt, d), lambda i: (i, 0)),
        out_shape=jax.ShapeDtypeStruct((t, d), F32),
        scratch_shapes=[pltpu.VMEM((2, TOP_K * tt // 8, 8, d), F32), pltpu.SemaphoreType.DMA((2,))],
        compiler_params=_cparams(("arbitrary",)),
        name="moe_combine",
    )(slots, slots, ybuf, xf, tw, g_ffn, final_gain.reshape(1, d))


def _route_plan(top_idx, top_pos, counts_f32, n_tok):
    counts = counts_f32[0, :N_EXPERTS].astype(jnp.int32)
    padded = (counts + EXPERT_BLOCK - 1) // EXPERT_BLOCK * EXPERT_BLOCK
    pad_end = jnp.cumsum(padded)
    pad_start = pad_end - padded
    n_blocks = -(-(n_tok * TOP_K) // EXPERT_BLOCK) + N_EXPERTS
    onehot = top_idx[:, :, None] == jnp.arange(N_EXPERTS, dtype=jnp.int32)
    tok_slots = top_pos + jnp.sum(jnp.where(onehot, pad_start, 0), axis=-1)
    block_start = jnp.arange(n_blocks, dtype=jnp.int32) * EXPERT_BLOCK
    block_expert = jnp.minimum(jnp.sum(pad_end[None, :] <= block_start[:, None], axis=1),
                               N_EXPERTS - 1).astype(jnp.int32)
    n_used = (pad_end[-1] // EXPERT_BLOCK).astype(jnp.int32).reshape(1)
    n_slots = n_blocks * EXPERT_BLOCK
    pad_len = padded - counts
    pad_cum = jnp.cumsum(pad_len)
    idx = jnp.arange(n_slots - n_tok * TOP_K, dtype=jnp.int32)
    owner = idx[:, None] >= pad_cum[None, :]
    first = jnp.logical_xor(owner, jnp.concatenate([jnp.ones_like(owner[:, :1]), owner[:, :-1]], axis=1))
    base = jnp.sum(jnp.where(first, (pad_start + counts) - (pad_cum - pad_len), 0), axis=1)
    free_slots = jnp.where(idx < pad_cum[-1], base + idx, pad_end[-1] + idx - pad_cum[-1])
    return (tok_slots.astype(jnp.int32), block_expert, n_used, free_slots.astype(jnp.int32), n_slots)


def _in_weight(w_in_l):
    lo0 = C_GA
    return jnp.concatenate([w_in_l[:, :lo0], w_in_l[:, lo0 + RW_LORA:], w_in_l[:, lo0:lo0 + RW_LORA]],
                           axis=1).astype(BF16)


def kernel(x, c, w_ada, b_ada, norm_mix, norm_moe, norm_final, w_in, conv_w, conv_b, lru_wa, lru_ba, lru_wx, lru_bx, lru_lambda, rw_mu, rw_w0, rw_w_up, rw_a0, rw_a_up, rw_g_up, rw_k_k, rw_k_a, rw_r_k, rw_lnx_w, rw_lnx_b, p_lru, p_sb, p_rwkv, w_out, w_router, b_router, w_gu, b_gu, w_down, b_down):
    bsz, seq, d = x.shape
    depth = w_in.shape[0]
    xf = x.reshape(bsz * seq, d)
    ada = _ada(c, w_ada, b_ada)
    for l in range(depth):
        sh_mix, sc_mix, g_mix, sh_ffn, sc_ffn, g_ffn = [
            ada[l, :, i * d:(i + 1) * d].reshape(bsz, 1, d) for i in range(N_ADA)]
        cols = _inproj(xf, norm_mix[l], sh_mix, sc_mix, _in_weight(w_in[l]), seq)
        ya = _rglru(cols, conv_w[l], conv_b[l], lru_wa[l], lru_ba[l], lru_wx[l], lru_bx[l],
                    lru_lambda[l], bsz, seq)
        yb = _sb_attention(cols, bsz, seq)
        yc = _rwkv(cols, rw_mu[l], rw_w0[l], rw_w_up[l], rw_a0[l], rw_a_up[l], rw_g_up[l],
                   rw_k_k[l], rw_k_a[l], rw_r_k[l], rw_lnx_w[l], rw_lnx_b[l], bsz, seq)
        xf, hffn, tw, ti, tp, cnt = _merge(xf, ya, yb, yc, cols, p_lru[l], p_sb[l], p_rwkv[l],
                                           w_out[l], g_mix, norm_moe[l], sh_ffn, sc_ffn,
                                           w_router[l], b_router[l], seq)
        tok_slots, block_expert, n_used, free_slots, n_slots = _route_plan(
            ti[:, :TOP_K], tp[:, :TOP_K], cnt, bsz * seq)
        xs = _dispatch(hffn, tok_slots, free_slots, n_slots)
        ybuf = _experts(xs, block_expert, n_used, w_gu, b_gu, w_down, b_down, l)
        xf = _combine(xf, ybuf, tok_slots, tw, g_ffn, norm_final, seq, final=(l == depth - 1))
    return xf.reshape(bsz, seq, d)
```

```python
import functools

import jax
import jax.numpy as jnp
from jax import lax
from jax.experimental import pallas as pl
from jax.experimental.pallas import tpu as pltpu

F32 = jnp.float32
BF16 = jnp.bfloat16

D_MODEL = 1024
NORM_EPS = 1e-6
N_ADA = 6

LRU_BLOCK_DIM = 64
CONV_WIDTH = 4
LRU_C = 8.0

SB_HEADS = 8
SB_HEAD_DIM = 128
SB_Q_BLOCK = 128

RW_HEAD_DIM = 64
RW_LORA = 256
RW_GN_EPS = 1e-5 * RW_HEAD_DIM
RW_CHUNK = 64
RW_GROUP = 256
RW_TILE = 512

N_EXPERTS = 32
TOP_K = 4
SWIGLU_LIMIT = 7.0
SWIGLU_ALPHA = 1.702
EXPERT_BLOCK = 512
EXPERT_FF_CHUNK = 512
DISPATCH_TOKENS = 256
COMBINE_TOKENS = 128

C_LX, C_LG, C_Q, C_K, C_V, C_RR, C_RK, C_RV, C_GA, C_GB, C_GC, C_LO = (
    0, 1024, 2048, 3072, 4096, 5120, 6144, 7168, 8192, 9216, 10240, 11264)

VMEM_LIMIT_BYTES = 56 * 1024 * 1024
F32_EXP_ZERO = -104.0


def _cparams(sem):
    return pltpu.CompilerParams(dimension_semantics=sem, vmem_limit_bytes=VMEM_LIMIT_BYTES)


def _dot(a, b):
    return jnp.dot(a, b, preferred_element_type=F32)


def _dot_nt(a, b):
    return lax.dot_general(a, b, (((1,), (1,)), ((), ())), preferred_element_type=F32)


def _split(x):
    hi = x.astype(BF16)
    lo = (x - hi.astype(F32)).astype(BF16)
    return hi, lo


def _softplus(x):
    return jnp.maximum(x, 0.0) + jnp.log(1.0 + jnp.exp(-jnp.abs(x)))


def _sigmoid(x):
    return 1.0 / (1.0 + jnp.exp(-x))


def _rmsnorm(x):
    return x * lax.rsqrt(jnp.mean(x * x, axis=-1, keepdims=True) + NORM_EPS)


def _ada_kernel(c_ref, w_ref, b_ref, o_ref):
    c = c_ref[...]
    ca = c * _sigmoid(c)
    o_ref[0] = jnp.dot(ca, w_ref[0], preferred_element_type=F32,
                       precision=lax.Precision.HIGHEST) + b_ref[0]


def _ada(c, w_ada, b_ada):
    depth, d, n = w_ada.shape
    bsz = c.shape[0]
    c8 = jnp.zeros((8, d), F32).at[:bsz].set(c)
    out = pl.pallas_call(
        _ada_kernel,
        grid=(depth, n // d),
        in_specs=[pl.BlockSpec((8, d), lambda l, j: (0, 0)),
                  pl.BlockSpec((1, d, d), lambda l, j: (l, 0, j)),
                  pl.BlockSpec((1, 1, d), lambda l, j: (l, 0, j))],
        out_specs=pl.BlockSpec((1, 8, d), lambda l, j: (l, 0, j)),
        out_shape=jax.ShapeDtypeStruct((depth, 8, n), F32),
        compiler_params=_cparams(("parallel", "parallel")),
        name="ada",
    )(c8, w_ada, b_ada.reshape(depth, 1, n))
    return out[:, :bsz]


def _inproj_kernel(x_ref, g_ref, sh_ref, sc_ref, w_ref, o_ref, h_ref):
    @pl.when(pl.program_id(1) == 0)
    def _():
        h = _rmsnorm(x_ref[...]) * g_ref[...] * (1.0 + sc_ref[0]) + sh_ref[0]
        h_ref[...] = h.astype(BF16)

    o_ref[...] = _dot(h_ref[...], w_ref[...]).astype(BF16)


def _inproj(xf, gain, shift, scale, w_bf16, seq):
    t, d = xf.shape
    n = w_bf16.shape[1]
    tm = min(1024, seq)
    tn = 1280
    per_b = seq // tm
    return pl.pallas_call(
        _inproj_kernel,
        grid=(t // tm, n // tn),
        in_specs=[pl.BlockSpec((tm, d), lambda i, j: (i, 0)),
                  pl.BlockSpec((1, d), lambda i, j: (0, 0)),
                  pl.BlockSpec((1, 1, d), lambda i, j: (i // per_b, 0, 0)),
                  pl.BlockSpec((1, 1, d), lambda i, j: (i // per_b, 0, 0)),
                  pl.BlockSpec((d, tn), lambda i, j: (0, j))],
        out_specs=pl.BlockSpec((tm, tn), lambda i, j: (i, j)),
        out_shape=jax.ShapeDtypeStruct((t, n), BF16),
        scratch_shapes=[pltpu.VMEM((tm, d), BF16)],
        compiler_params=_cparams(("parallel", "arbitrary")),
        name="inproj",
    )(xf, gain.reshape(1, d), shift, scale, w_bf16)


def _rglru_kernel(x_ref, gate_ref, cw_ref, cb_ref, wax_ref, bax_ref, lam_ref, o_ref,
                  tail_ref, h_ref):
    ts, d = x_ref.shape

    @pl.when(pl.program_id(1) == 0)
    def _():
        tail_ref[...] = jnp.zeros_like(tail_ref)
        h_ref[...] = jnp.zeros_like(h_ref)

    x = x_ref[...].astype(F32)
    tail = tail_ref[...]
    row8 = lax.broadcasted_iota(jnp.int32, (8, d), 0)
    xc = cb_ref[...] + cw_ref[0:1, :] * x
    for k in range(1, CONV_WIDTH):
        xs = pltpu.roll(x, k, 0)
        head = jnp.where(row8 < k, pltpu.roll(tail, k, 0), xs[0:8])
        xs = jnp.concatenate([head, xs[8:]], axis=0)
        xc = xc + cw_ref[k:k + 1, :] * xs
    tail_ref[...] = x[ts - 8:]

    rs, is_ = [], []
    for blk in range(d // 128):
        xb = xc[:, blk * 128:(blk + 1) * 128].astype(BF16)
        ri = _dot(xb, wax_ref[blk]) + bax_ref[blk]
        rs.append(ri[:, :128])
        is_.append(ri[:, 128:])
    r = _sigmoid(jnp.concatenate(rs, axis=1))
    i = _sigmoid(jnp.concatenate(is_, axis=1))
    log_a = (-LRU_C) * r * _softplus(-lam_ref[...])
    a = jnp.exp(log_a)
    b = jnp.sqrt(1.0 - jnp.exp(2.0 * log_a)) * (i * xc)

    row = lax.broadcasted_iota(jnp.int32, (ts, d), 0)
    sh = 1
    while sh < ts:
        a_s = pltpu.roll(a, sh, 0)
        b_s = pltpu.roll(b, sh, 0)
        m = row >= sh
        b = jnp.where(m, a * b_s + b, b)
        a = jnp.where(m, a * a_s, a)
        sh *= 2
    h = a * h_ref[...] + b
    h_ref[...] = h[ts - 1:]

    g = gate_ref[...].astype(F32)
    gelu = 0.5 * g * (1.0 + jnp.tanh(0.7978845608028654 * (g + 0.044715 * g * g * g)))
    o_ref[...] = (gelu * h).astype(BF16)


def _rglru(cols, conv_w, conv_b, wa, ba, wx, bx, lam, bsz, seq):
    t = cols.shape[0]
    d = D_MODEL
    ts = min(256, seq)
    per_b = seq // ts
    nb = d // 128

    def bdiag(w):
        w = w.reshape(nb, 2, LRU_BLOCK_DIM, LRU_BLOCK_DIM)
        z = jnp.zeros((nb, LRU_BLOCK_DIM, LRU_BLOCK_DIM), w.dtype)
        top = jnp.concatenate([w[:, 0], z], axis=2)
        bot = jnp.concatenate([z, w[:, 1]], axis=2)
        return jnp.concatenate([top, bot], axis=1)

    wax = jnp.concatenate([bdiag(wa), bdiag(wx)], axis=2).astype(BF16)
    bax = jnp.concatenate([ba.reshape(nb, 1, 128), bx.reshape(nb, 1, 128)], axis=2)
    return pl.pallas_call(
        _rglru_kernel,
        grid=(bsz, per_b),
        in_specs=[pl.BlockSpec((ts, d), lambda b, s: (b * per_b + s, C_LX // d)),
                  pl.BlockSpec((ts, d), lambda b, s: (b * per_b + s, C_LG // d)),
                  pl.BlockSpec((CONV_WIDTH, d), lambda b, s: (0, 0)),
                  pl.BlockSpec((1, d), lambda b, s: (0, 0)),
                  pl.BlockSpec((nb, 128, 256), lambda b, s: (0, 0, 0)),
                  pl.BlockSpec((nb, 1, 256), lambda b, s: (0, 0, 0)),
                  pl.BlockSpec((1, d), lambda b, s: (0, 0))],
        out_specs=pl.BlockSpec((ts, d), lambda b, s: (b * per_b + s, 0)),
        out_shape=jax.ShapeDtypeStruct((t, d), BF16),
        scratch_shapes=[pltpu.VMEM((8, d), F32), pltpu.VMEM((1, d), F32)],
        compiler_params=_cparams(("parallel", "arbitrary")),
        name="rglru",
    )(cols, cols, conv_w, conv_b.reshape(1, d), wax, bax, lam.reshape(1, d))


def _sb_kernel(q_ref, k_ref, v_ref, o_ref, acc_ref, carry_ref):
    tb = SB_Q_BLOCK
    nqb = q_ref.shape[0] // tb
    dh = SB_HEAD_DIM
    nh = q_ref.shape[1] // dh
    first = pl.program_id(2) * nqb
    scale = dh ** -0.5
    acc_ref[...] = jnp.zeros_like(acc_ref)
    carry_ref[...] = jnp.zeros_like(carry_ref)
    rowi = lax.broadcasted_iota(jnp.int32, (tb, tb), 0)
    coli = lax.broadcasted_iota(jnp.int32, (tb, tb), 1)
    valid = coli < rowi
    upper = (rowi > coli).astype(BF16)

    def block(n, masked, qblocks):
        st = [(u, h) for u in qblocks for h in range(nh)]
        qrow = {u: slice(u * tb, (u + 1) * tb) for u in qblocks}
        krow = {u: pl.ds(pl.multiple_of((first + u - n) * tb, tb), tb) for u in qblocks}
        lane = [slice(h * dh, (h + 1) * dh) for h in range(nh)]
        z = [_dot_nt(q_ref[qrow[u], lane[h]], k_ref[krow[u], lane[h]]) * scale for u, h in st]
        sp = [_softplus(x) for x in z]
        if masked:
            log_stay = [jnp.where(valid, -x, 0.0) for x in sp]
        else:
            log_stay = [-x for x in sp]
        parts = [_split(x) for x in log_stay]
        carry = [carry_ref[u * nh + h] for u, h in st]
        later = [_dot(parts[i][0], upper) + _dot(parts[i][1], upper) + carry[i] for i in range(len(st))]
        attn = [jnp.exp(z[i] - sp[i] + later[i]) for i in range(len(st))]
        if masked:
            attn = [jnp.where(valid, x, 0.0) for x in attn]
        for i, (u, h) in enumerate(st):
            acc_ref[qrow[u], lane[h]] += _dot(attn[i].astype(BF16), v_ref[krow[u], lane[h]])
        carry = [carry[i] + jnp.sum(log_stay[i], axis=1, keepdims=True) for i in range(len(st))]
        top = carry[0]
        for i, (u, h) in enumerate(st):
            carry_ref[u * nh + h] = carry[i]
            top = jnp.maximum(top, carry[i])
        return jnp.max(top) >= F32_EXP_ZERO

    everyone = list(range(nqb))

    def body(state):
        n, _ = state
        return n + 1, block(n, False, everyone)

    def cond(state):
        n, go = state
        return jnp.logical_and(n <= first, go)

    n_end, go = lax.while_loop(cond, body, (1, block(0, True, everyone)))
    if nqb == 2:
        @pl.when(jnp.logical_and(go, n_end == first + 1))
        def _():
            block(first + 1, False, [1])
    o_ref[...] = acc_ref[...].astype(BF16)


def _sb_attention(cols, bsz, seq):
    t = cols.shape[0]
    dh = SB_HEAD_DIM
    hg = 4
    gw = hg * dh
    tq = min(2 * SB_Q_BLOCK, seq)
    nq = seq // tq
    qb, kb, vb = C_Q // gw, C_K // gw, C_V // gw
    return pl.pallas_call(
        _sb_kernel,
        grid=(bsz, SB_HEADS // hg, nq),
        in_specs=[pl.BlockSpec((tq, gw), lambda b, g, i: (b * nq + i, qb + g)),
                  pl.BlockSpec((seq, gw), lambda b, g, i: (b, kb + g)),
                  pl.BlockSpec((seq, gw), lambda b, g, i: (b, vb + g))],
        out_specs=pl.BlockSpec((tq, gw), lambda b, g, i: (b * nq + i, g)),
        out_shape=jax.ShapeDtypeStruct((t, SB_HEADS * dh), BF16),
        scratch_shapes=[pltpu.VMEM((tq, gw), F32),
                        pltpu.VMEM((tq // SB_Q_BLOCK * hg, SB_Q_BLOCK, 1), F32)],
        compiler_params=_cparams(("parallel", "parallel", "arbitrary")),
        name="sb_attention",
    )(cols, cols, cols)


def _rwkv_kernel(r_ref, k_ref, v_ref, l_ref, mu_ref, par_ref, lw_ref, o_ref, st_ref, prev_ref):
    ts, gw = r_ref.shape
    c = RW_CHUNK
    nch = ts // c
    n = RW_HEAD_DIM
    nh = gw // n

    @pl.when(pl.program_id(2) == 0)
    def _():
        st_ref[...] = jnp.zeros_like(st_ref)
        prev_ref[...] = jnp.zeros_like(prev_ref)

    row = lax.broadcasted_iota(jnp.int32, (ts, gw), 0)

    def shift_mix(ref, idx):
        x = ref[...].astype(F32)
        prev = jnp.where(row == 0, prev_ref[idx:idx + 1, :], pltpu.roll(x, 1, 0))
        prev_ref[idx:idx + 1, :] = x[ts - 1:]
        return x + mu_ref[idx:idx + 1, :] * (prev - x)

    r = shift_mix(r_ref, 0)
    k = shift_mix(k_ref, 1)
    v = shift_mix(v_ref, 2)
    lo = shift_mix(l_ref, 3)

    w0, a0 = par_ref[0:1, :], par_ref[1:2, :]
    k_k, k_a, r_k = par_ref[2:3, :], par_ref[3:4, :], par_ref[4:5, :]
    lnx_w, lnx_b = par_ref[5:6, :], par_ref[6:7, :]

    dw = _dot(jnp.tanh(lo).astype(BF16), lw_ref[0])
    da = _dot(lo.astype(BF16), lw_ref[1])
    g = _dot(_sigmoid(lo).astype(BF16), lw_ref[2])
    w_log = -_softplus(-(w0 + dw)) - 0.5
    ld = -jnp.exp(w_log)
    rate = _sigmoid(a0 + da)

    ri = lax.broadcasted_iota(jnp.int32, (gw, gw), 0)
    ci = lax.broadcasted_iota(jnp.int32, (gw, gw), 1)
    shift_n = n.bit_length() - 1
    same_blk = (ri >> shift_n) == (ci >> shift_n)
    seg = same_blk.astype(BF16)
    tri = (lax.broadcasted_iota(jnp.int32, (c, c), 1)
           <= lax.broadcasted_iota(jnp.int32, (c, c), 0)).astype(BF16)

    def segsum(x):
        hi, lo_ = _split(x)
        return _dot(hi, seg) + _dot(lo_, seg)

    kk = k * k_k
    kk = kk * lax.rsqrt(jnp.maximum(segsum(kk * kk), 1e-24))
    k2 = k * (1.0 + (rate - 1.0) * k_a)
    av = -kk
    bv = kk * rate

    ld_hi, ld_lo = _split(ld)

    t_i = lax.broadcasted_iota(jnp.int32, (c, gw), 0)
    s_i = lax.broadcasted_iota(jnp.int32, (c, gw), 1) & (c - 1)
    strict = s_i < t_i
    incl = s_i <= t_i
    eye = (s_i == t_i).astype(F32)

    def stacked(x):
        xb = x.astype(BF16)
        return jnp.where(same_blk, jnp.concatenate([xb] * nh, axis=0), jnp.zeros((), BF16))

    chs = range(nch)
    sls = [slice(ch * c, (ch + 1) * c) for ch in chs]
    cl_c = [_dot(tri, ld_hi[sl]) + _dot(tri, ld_lo[sl]) for sl in sls]
    cl_end = [x[c - 1:c] for x in cl_c]
    e_pos = [jnp.exp(x) for x in cl_c]
    e_neg = [jnp.exp(-x) for x in cl_c]
    e_end = [jnp.exp(cl_end[ch] - cl_c[ch]) for ch in chs]
    rt = [r[sls[ch]] * e_pos[ch] for ch in chs]
    at = [av[sls[ch]] * jnp.exp(cl_c[ch] - ld[sls[ch]]) for ch in chs]
    bt = [bv[sls[ch]] * e_neg[ch] for ch in chs]
    kt = [k2[sls[ch]] * e_neg[ch] for ch in chs]
    bh = [bv[sls[ch]] * e_end[ch] for ch in chs]
    kh = [k2[sls[ch]] * e_end[ch] for ch in chs]
    vc = [v[sl] for sl in sls]
    gam = [jnp.exp(x) for x in cl_end]

    ar = [jnp.concatenate([at[ch], rt[ch]], axis=0).astype(BF16) for ch in chs]
    ab = [_dot_nt(ar[ch], stacked(bt[ch])) for ch in chs]
    ak = [_dot_nt(ar[ch], stacked(kt[ch])) for ch in chs]
    a_m = [jnp.where(strict, x[:c], 0.0) for x in ab]
    pb = [jnp.where(incl, x[c:], 0.0).astype(BF16) for x in ab]
    ak_m = [jnp.where(strict, x[:c], 0.0).astype(BF16) for x in ak]
    pk = [jnp.where(incl, x[c:], 0.0).astype(BF16) for x in ak]

    tinv = [eye + x for x in a_m]
    apow = a_m
    p = 2
    while p < c:
        apow = [_dot(x.astype(BF16), stacked(x)) for x in apow]
        tinv = [tinv[ch] + _dot(tinv[ch].astype(BF16), stacked(apow[ch])) for ch in chs]
        p *= 2
    tb = [x.astype(BF16) for x in tinv]

    vs = [stacked(x) for x in vc]
    w = [_dot(tb[ch], stacked(at[ch])) for ch in chs]
    akv = [_dot(ak_m[ch], vs[ch]) for ch in chs]
    z = [_dot(tb[ch], stacked(akv[ch])) for ch in chs]
    rhat = [(rt[ch] + _dot(pb[ch], stacked(w[ch]))).astype(BF16) for ch in chs]
    yhat = [_dot(pb[ch], stacked(z[ch])) + _dot(pk[ch], vs[ch]) for ch in chs]
    zero = jnp.zeros((c, gw), F32)
    lt = [jnp.concatenate([w[ch], z[ch], vc[ch], zero], axis=0).T.astype(BF16) for ch in chs]
    x1 = [jnp.where(same_blk, _dot(lt[ch], jnp.concatenate([bh[ch], zero, zero, zero], 0).astype(BF16)),
                    0.0).astype(BF16) for ch in chs]
    x2 = [jnp.where(same_blk, _dot(lt[ch], jnp.concatenate([zero, bh[ch], kh[ch], zero], 0).astype(BF16)),
                    0.0) for ch in chs]

    st = st_ref[...]
    ys = []
    for ch in chs:
        stb = st.astype(BF16)
        ys.append(_dot_nt(rhat[ch], stb) + yhat[ch])
        st = gam[ch] * st + _dot(stb, x1[ch]) + x2[ch]
    st_ref[...] = st
    y = jnp.concatenate(ys, axis=0)

    inv_n = 1.0 / n
    mean = segsum(y) * inv_n
    dev = y - mean
    var = segsum(dev * dev) * inv_n
    yn = dev * lax.rsqrt(var + RW_GN_EPS) * lnx_w + lnx_b
    bonus = segsum(r * k2 * r_k) * v
    o_ref[...] = ((yn + bonus) * g).astype(BF16)


def _rwkv(cols, mu, w0, w_up, a0, a_up, g_up, k_k, k_a, r_k, lnx_w, lnx_b, bsz, seq):
    t = cols.shape[0]
    d = D_MODEL
    gw = RW_GROUP
    ts = min(RW_TILE, seq)
    per_b = seq // ts
    ng = d // gw
    mu4 = jnp.stack([mu[0:d], mu[d:2 * d], mu[2 * d:3 * d],
                     jnp.tile(mu[3 * d:], d // RW_LORA)], axis=0)
    par = jnp.stack([w0, a0, k_k, k_a, r_k.reshape(d), lnx_w, lnx_b, jnp.zeros((d,), F32)], axis=0)
    lw = jnp.zeros((3, RW_LORA, d), F32)
    lw = lw.at[0, 0:64].set(w_up).at[1, 64:128].set(a_up).at[2, 128:256].set(g_up).astype(BF16)
    rb, kb, vb, lb = C_RR // gw, C_RK // gw, C_RV // gw, C_LO // gw
    return pl.pallas_call(
        _rwkv_kernel,
        grid=(bsz, ng, per_b),
        in_specs=[pl.BlockSpec((ts, gw), lambda b, g, s: (b * per_b + s, rb + g)),
                  pl.BlockSpec((ts, gw), lambda b, g, s: (b * per_b + s, kb + g)),
                  pl.BlockSpec((ts, gw), lambda b, g, s: (b * per_b + s, vb + g)),
                  pl.BlockSpec((ts, gw), lambda b, g, s: (b * per_b + s, lb)),
                  pl.BlockSpec((4, gw), lambda b, g, s: (0, g)),
                  pl.BlockSpec((8, gw), lambda b, g, s: (0, g)),
                  pl.BlockSpec((3, RW_LORA, gw), lambda b, g, s: (0, 0, g))],
        out_specs=pl.BlockSpec((ts, gw), lambda b, g, s: (b * per_b + s, g)),
        out_shape=jax.ShapeDtypeStruct((t, d), BF16),
        scratch_shapes=[pltpu.VMEM((gw, gw), F32), pltpu.VMEM((8, gw), F32)],
        compiler_params=_cparams(("parallel", "parallel", "arbitrary")),
        name="rwkv7",
    )(cols, cols, cols, cols, mu4, par, lw)


def _merge_kernel(x_ref, ya_ref, yb_ref, yc_ref, ga_ref, gb_ref, gc_ref, pa_ref, pb_ref, pc_ref,
                  wo_ref, gm_ref, gn_ref, sh_ref, sc_ref, wrh_ref, wrl_ref, br_ref,
                  xo_ref, h_ref, tw_ref, ti_ref, tp_ref, cnt_ref, run_ref):
    @pl.when(pl.program_id(0) == 0)
    def _():
        run_ref[...] = jnp.zeros_like(run_ref)

    m = _sigmoid(ga_ref[...].astype(F32)) * _dot(ya_ref[...], pa_ref[...])
    m = m + _sigmoid(gb_ref[...].astype(F32)) * _dot(yb_ref[...], pb_ref[...])
    m = m + _sigmoid(gc_ref[...].astype(F32)) * _dot(yc_ref[...], pc_ref[...])
    xn = x_ref[...] + gm_ref[0] * _dot(m.astype(BF16), wo_ref[...])
    xo_ref[...] = xn
    h = _rmsnorm(xn) * gn_ref[...] * (1.0 + sc_ref[0]) + sh_ref[0]
    h_ref[...] = h

    hh, hl = _split(h)
    logits = _dot(hh, wrh_ref[...]) + _dot(hh, wrl_ref[...]) + _dot(hl, wrh_ref[...]) + br_ref[...]
    lane = lax.broadcasted_iota(jnp.int32, logits.shape, 1)
    lanes = logits.shape[1]
    tw = jnp.zeros(logits.shape, F32)
    ti = jnp.zeros(logits.shape, jnp.int32)
    top = None
    wsum = None
    es, sel = [], []
    for kth in range(TOP_K):
        mk = jnp.max(logits, axis=1, keepdims=True)
        ik = jnp.min(jnp.where(logits == mk, lane, lanes), axis=1, keepdims=True)
        hit = lane == ik
        logits = jnp.where(hit, -jnp.inf, logits)
        if kth == 0:
            top = mk
        e = jnp.exp(mk - top)
        es.append(e)
        sel.append(hit)
        wsum = e if wsum is None else wsum + e
        ti = jnp.where(lane == kth, ik, ti)
    inv = 1.0 / wsum
    for kth in range(TOP_K):
        tw = jnp.where(lane == kth, es[kth] * inv, tw)
    tw_ref[...] = tw
    ti_ref[...] = ti

    multi = jnp.zeros(tw.shape, F32)
    for hit in sel:
        multi = multi + hit.astype(F32)
    rows = tw.shape[0]
    earlier = (lax.broadcasted_iota(jnp.int32, (rows, rows), 1)
               < lax.broadcasted_iota(jnp.int32, (rows, rows), 0)).astype(BF16)
    before = _dot(earlier, multi.astype(BF16)) + run_ref[...]
    tp = jnp.zeros(tw.shape, jnp.int32)
    for kth in range(TOP_K):
        pos = jnp.sum(jnp.where(sel[kth], before, 0.0), axis=1, keepdims=True)
        tp = jnp.where(lane == kth, pos.astype(jnp.int32), tp)
    tp_ref[...] = tp
    run_ref[...] += jnp.sum(multi, axis=0, keepdims=True)
    cnt_ref[...] = run_ref[...]


def _merge(xf, ya, yb, yc, cols, p_lru, p_sb, p_rwkv, w_out, g_mix, gain, shift, scale,
           w_router, b_router, seq):
    t, d = xf.shape
    tm = min(512, seq)
    per_b = seq // tm
    ne = w_router.shape[1]
    wr = jnp.zeros((d, 128), F32).at[:, :ne].set(w_router)
    wrh, wrl = _split(wr)
    br = jnp.full((1, 128), -1e30, F32).at[0, :ne].set(b_router)
    row = lambda i: (i, 0)
    const = lambda i: (0, 0)
    perb = lambda i: (i // per_b, 0, 0)
    return pl.pallas_call(
        _merge_kernel,
        grid=(t // tm,),
        in_specs=[pl.BlockSpec((tm, d), row), pl.BlockSpec((tm, d), row),
                  pl.BlockSpec((tm, d), row), pl.BlockSpec((tm, d), row),
                  pl.BlockSpec((tm, d), lambda i: (i, C_GA // d)),
                  pl.BlockSpec((tm, d), lambda i: (i, C_GB // d)),
                  pl.BlockSpec((tm, d), lambda i: (i, C_GC // d)),
                  pl.BlockSpec((d, d), const), pl.BlockSpec((d, d), const),
                  pl.BlockSpec((d, d), const), pl.BlockSpec((d, d), const),
                  pl.BlockSpec((1, 1, d), perb), pl.BlockSpec((1, d), const),
                  pl.BlockSpec((1, 1, d), perb), pl.BlockSpec((1, 1, d), perb),
                  pl.BlockSpec((d, 128), const), pl.BlockSpec((d, 128), const),
                  pl.BlockSpec((1, 128), const)],
        out_specs=[pl.BlockSpec((tm, d), row), pl.BlockSpec((tm, d), row),
                   pl.BlockSpec((tm, 128), row), pl.BlockSpec((tm, 128), row),
                   pl.BlockSpec((tm, 128), row), pl.BlockSpec((1, 128), const)],
        out_shape=[jax.ShapeDtypeStruct((t, d), F32), jax.ShapeDtypeStruct((t, d), F32),
                   jax.ShapeDtypeStruct((t, 128), F32), jax.ShapeDtypeStruct((t, 128), jnp.int32),
                   jax.ShapeDtypeStruct((t, 128), jnp.int32), jax.ShapeDtypeStruct((1, 128), F32)],
        scratch_shapes=[pltpu.VMEM((1, 128), F32)],
        compiler_params=_cparams(("arbitrary",)),
        name="merge_router",
    )(xf, ya, yb, yc, cols, cols, cols, p_lru.astype(BF16), p_sb.astype(BF16), p_rwkv.astype(BF16),
      w_out.astype(BF16), g_mix, gain.reshape(1, d), shift, scale, wrh, wrl, br)


def _row_copy(src, dst, sem, src_row, dst_row):
    return pltpu.make_async_copy(src.at[pl.ds(src_row, 1)], dst.at[pl.ds(dst_row, 1)], sem)


def _wait_rows(src, dst, sem, n_rows):
    def wait(i, carry):
        _row_copy(src, dst, sem, 0, 0).wait()
        return carry

    lax.fori_loop(0, n_rows, wait, 0, unroll=8)


def _dispatch_kernel(slot_ref, free_ref, h_ref, xs_out, zrow, sem):
    n_tok = h_ref.shape[0]
    n_free = free_ref.shape[2]

    @pl.when(pl.program_id(0) == 0)
    def _():
        zrow[...] = jnp.zeros_like(zrow)

    def start(g, carry):
        base = pl.multiple_of(g * 8, 8)
        for s in range(8):
            for kth in range(TOP_K):
                a = s * TOP_K + kth
                slot = slot_ref[0, 0, g * (8 * TOP_K) + a]
                _row_copy(h_ref, xs_out, sem, base + s, slot).start(priority=a % 2)
        return carry

    lax.fori_loop(0, n_tok // 8, start, 0)

    def start_free(g, carry):
        for s in range(8):
            _row_copy(zrow, xs_out, sem, 0, free_ref[0, 0, g * 8 + s]).start(priority=s % 2)
        return carry

    lax.fori_loop(0, n_free // 8, start_free, 0)
    _wait_rows(h_ref, xs_out, sem, n_tok * TOP_K + n_free)


def _dispatch(hffn, tok_slots, free_slots, n_slots):
    t, d = hffn.shape
    tt = min(DISPATCH_TOKENS, t)
    nt = t // tt
    n_free = free_slots.shape[0] // nt
    return pl.pallas_call(
        _dispatch_kernel,
        grid=(nt,),
        in_specs=[pl.BlockSpec((1, 1, TOP_K * tt), lambda i: (i, 0, 0), memory_space=pltpu.SMEM),
                  pl.BlockSpec((1, 1, n_free), lambda i: (i, 0, 0), memory_space=pltpu.SMEM),
                  pl.BlockSpec((tt, d), lambda i: (i, 0))],
        out_specs=pl.BlockSpec(memory_space=pl.ANY),
        out_shape=jax.ShapeDtypeStruct((n_slots, d), F32),
        scratch_shapes=[pltpu.VMEM((8, d), F32), pltpu.SemaphoreType.DMA(())],
        compiler_params=_cparams(("arbitrary",)),
        name="moe_dispatch",
    )(tok_slots.reshape(nt, 1, TOP_K * tt), free_slots.reshape(nt, 1, n_free), hffn)


def _expert_kernel(be_ref, nused_ref, x_ref, wgu_ref, bgu_ref, wd_ref, bd_ref, o_ref,
                   wgu_bf, wd_bf):
    blk = pl.program_id(0)
    dff = wd_ref.shape[2]
    used = blk < nused_ref[0]
    fresh = jnp.logical_or(blk == 0, be_ref[blk] != be_ref[jnp.maximum(blk - 1, 0)])

    @pl.when(jnp.logical_and(used, fresh))
    def _():
        wgu_bf[...] = wgu_ref[0, 0].astype(BF16)
        wd_bf[...] = wd_ref[0, 0].astype(BF16)

    @pl.when(used)
    def _():
        xb = x_ref[...].astype(BF16)
        y = bd_ref[0, 0]
        for lo in range(0, dff, EXPERT_FF_CHUNK):
            hi = lo + EXPERT_FF_CHUNK
            gate = _dot(xb, wgu_bf[:, lo:hi]) + bgu_ref[0, 0, :, lo:hi]
            up = _dot(xb, wgu_bf[:, dff + lo:dff + hi]) + bgu_ref[0, 0, :, dff + lo:dff + hi]
            gate = jnp.minimum(gate, SWIGLU_LIMIT)
            up = jnp.clip(up, -SWIGLU_LIMIT, SWIGLU_LIMIT)
            act = (up + 1.0) * gate * _sigmoid(SWIGLU_ALPHA * gate)
            y = y + _dot(act.astype(BF16), wd_bf[lo:hi, :])
        o_ref[...] = y

    @pl.when(jnp.logical_not(used))
    def _():
        o_ref[...] = jnp.zeros_like(o_ref)


def _experts(xs, block_expert, n_used, w_gu, b_gu, w_down, b_down, layer):
    d = xs.shape[1]
    n_blocks = block_expert.shape[0]
    depth, ne, _, dff2 = w_gu.shape
    dff = dff2 // 2
    eb = EXPERT_BLOCK
    grid_spec = pltpu.PrefetchScalarGridSpec(
        num_scalar_prefetch=2,
        grid=(n_blocks,),
        in_specs=[pl.BlockSpec((eb, d), lambda b, be, nu: (jnp.minimum(b, nu[0] - 1), 0)),
                  pl.BlockSpec((1, 1, d, dff2), lambda b, be, nu: (layer, be[b], 0, 0)),
                  pl.BlockSpec((1, 1, 1, dff2), lambda b, be, nu: (layer, be[b], 0, 0)),
                  pl.BlockSpec((1, 1, dff, d), lambda b, be, nu: (layer, be[b], 0, 0)),
                  pl.BlockSpec((1, 1, 1, d), lambda b, be, nu: (layer, be[b], 0, 0))],
        out_specs=pl.BlockSpec((eb, d), lambda b, be, nu: (b, 0)),
        scratch_shapes=[pltpu.VMEM((d, dff2), BF16), pltpu.VMEM((dff, d), BF16)],
    )
    return pl.pallas_call(
        _expert_kernel,
        grid_spec=grid_spec,
        out_shape=jax.ShapeDtypeStruct((n_blocks * eb, d), F32),
        compiler_params=_cparams(("arbitrary",)),
        name="moe_experts",
    )(block_expert, n_used, xs, w_gu, b_gu.reshape(depth, ne, 1, dff2), w_down,
      b_down.reshape(depth, ne, 1, d))


def _combine_kernel(slot_ref, slot_next_ref, yb_hbm, x_ref, tw_ref, gf_ref, gain_ref, o_ref,
                    buf, sem, *, final):
    tt = x_ref.shape[0]
    n_rows = TOP_K * tt
    i = pl.program_id(0)
    cur = i % 2

    groups = tt // 8

    def start_gather(idx_ref, slot):
        def start(g, carry):
            for kth in range(TOP_K):
                dst = buf.at[slot, kth * groups + g]
                for s in range(8):
                    _row_copy(yb_hbm, dst, sem.at[slot], idx_ref[0, 0, kth * tt + g * 8 + s],
                              s).start(priority=s % 2)
            return carry

        lax.fori_loop(0, groups, start, 0)

    @pl.when(i == 0)
    def _():
        start_gather(slot_ref, 0)

    @pl.when(i + 1 < pl.num_programs(0))
    def _():
        start_gather(slot_next_ref, 1 - cur)

    _wait_rows(yb_hbm, buf.at[cur, 0], sem.at[cur], n_rows)
    d = x_ref.shape[1]
    ffn = tw_ref[:, 0:1] * buf[cur, 0:groups].reshape(tt, d)
    for kth in range(1, TOP_K):
        ffn = ffn + tw_ref[:, kth:kth + 1] * buf[cur, kth * groups:(kth + 1) * groups].reshape(tt, d)
    xn = x_ref[...] + gf_ref[0] * ffn
    if final:
        xn = _rmsnorm(xn) * gain_ref[...]
    o_ref[...] = xn


def _combine(xf, ybuf, tok_slots, tw, g_ffn, final_gain, seq, final):
    t, d = xf.shape
    tt = min(COMBINE_TOKENS, seq)
    per_b = seq // tt
    nt = t // tt
    slots = tok_slots.reshape(nt, tt, TOP_K).transpose(0, 2, 1).reshape(nt, 1, TOP_K * tt)
    return pl.pallas_call(
        functools.partial(_combine_kernel, final=final),
        grid=(nt,),
        in_specs=[pl.BlockSpec((1, 1, TOP_K * tt), lambda i: (i, 0, 0), memory_space=pltpu.SMEM),
                  pl.BlockSpec((1, 1, TOP_K * tt), lambda i: (jnp.minimum(i + 1, nt - 1), 0, 0),
                               memory_space=pltpu.SMEM),
                  pl.BlockSpec(memory_space=pl.ANY),
                  pl.BlockSpec((tt, d), lambda i: (i, 0)),
                  pl.BlockSpec((tt, 128), lambda i: (i, 0)),
                  pl.BlockSpec((1, 1, d), lambda i: (i // per_b, 0, 0)),
                  pl.BlockSpec((1, d), lambda i: (0, 0))],
        out_specs=pl.BlockSpec((tt, d), lambda i: (i, 0)),
        out_shape=jax.ShapeDtypeStruct((t, d), F32),
        scratch_shapes=[pltpu.VMEM((2, TOP_K * tt // 8, 8, d), F32), pltpu.SemaphoreType.DMA((2,))],
        compiler_params=_cparams(("arbitrary",)),
        name="moe_combine",
    )(slots, slots, ybuf, xf, tw, g_ffn, final_gain.reshape(1, d))


def _route_plan(top_idx, top_pos, counts_f32, n_tok):
    counts = counts_f32[0, :N_EXPERTS].astype(jnp.int32)
    padded = (counts + EXPERT_BLOCK - 1) // EXPERT_BLOCK * EXPERT_BLOCK
    pad_end = jnp.cumsum(padded)
    pad_start = pad_end - padded
    n_blocks = -(-(n_tok * TOP_K) // EXPERT_BLOCK) + N_EXPERTS
    onehot = top_idx[:, :, None] == jnp.arange(N_EXPERTS, dtype=jnp.int32)
    tok_slots = top_pos + jnp.sum(jnp.where(onehot, pad_start, 0), axis=-1)
    block_start = jnp.arange(n_blocks, dtype=jnp.int32) * EXPERT_BLOCK
    block_expert = jnp.minimum(jnp.sum(pad_end[None, :] <= block_start[:, None], axis=1),
                               N_EXPERTS - 1).astype(jnp.int32)
    n_used = (pad_end[-1] // EXPERT_BLOCK).astype(jnp.int32).reshape(1)
    n_slots = n_blocks * EXPERT_BLOCK
    pad_len = padded - counts
    pad_cum = jnp.cumsum(pad_len)
    idx = jnp.arange(n_slots - n_tok * TOP_K, dtype=jnp.int32)
    owner = idx[:, None] >= pad_cum[None, :]
    first = jnp.logical_xor(owner, jnp.concatenate([jnp.ones_like(owner[:, :1]), owner[:, :-1]], axis=1))
    base = jnp.sum(jnp.where(first, (pad_start + counts) - (pad_cum - pad_len), 0), axis=1)
    free_slots = jnp.where(idx < pad_cum[-1], base + idx, pad_end[-1] + idx - pad_cum[-1])
    return (tok_slots.astype(jnp.int32), block_expert, n_used, free_slots.astype(jnp.int32), n_slots)


def _in_weight(w_in_l):
    lo0 = C_GA
    return jnp.concatenate([w_in_l[:, :lo0], w_in_l[:, lo0 + RW_LORA:], w_in_l[:, lo0:lo0 + RW_LORA]],
                           axis=1).astype(BF16)


def kernel(x, c, w_ada, b_ada, norm_mix, norm_moe, norm_final, w_in, conv_w, conv_b, lru_wa, lru_ba, lru_wx, lru_bx, lru_lambda, rw_mu, rw_w0, rw_w_up, rw_a0, rw_a_up, rw_g_up, rw_k_k, rw_k_a, rw_r_k, rw_lnx_w, rw_lnx_b, p_lru, p_sb, p_rwkv, w_out, w_router, b_router, w_gu, b_gu, w_down, b_down):
    bsz, seq, d = x.shape
    depth = w_in.shape[0]
    xf = x.reshape(bsz * seq, d)
    ada = _ada(c, w_ada, b_ada)
    for l in range(depth):
        sh_mix, sc_mix, g_mix, sh_ffn, sc_ffn, g_ffn = [
            ada[l, :, i * d:(i + 1) * d].reshape(bsz, 1, d) for i in range(N_ADA)]
        cols = _inproj(xf, norm_mix[l], sh_mix, sc_mix, _in_weight(w_in[l]), seq)
        ya = _rglru(cols, conv_w[l], conv_b[l], lru_wa[l], lru_ba[l], lru_wx[l], lru_bx[l],
                    lru_lambda[l], bsz, seq)
        yb = _sb_attention(cols, bsz, seq)
        yc = _rwkv(cols, rw_mu[l], rw_w0[l], rw_w_up[l], rw_a0[l], rw_a_up[l], rw_g_up[l],
                   rw_k_k[l], rw_k_a[l], rw_r_k[l], rw_lnx_w[l], rw_lnx_b[l], bsz, seq)
        xf, hffn, tw, ti, tp, cnt = _merge(xf, ya, yb, yc, cols, p_lru[l], p_sb[l], p_rwkv[l],
                                           w_out[l], g_mix, norm_moe[l], sh_ffn, sc_ffn,
                                           w_router[l], b_router[l], seq)
        tok_slots, block_expert, n_used, free_slots, n_slots = _route_plan(
            ti[:, :TOP_K], tp[:, :TOP_K], cnt, bsz * seq)
        xs = _dispatch(hffn, tok_slots, free_slots, n_slots)
        ybuf = _experts(xs, block_expert, n_used, w_gu, b_gu, w_down, b_down, l)
        xf = _combine(xf, ybuf, tok_slots, tw, g_ffn, norm_final, seq, final=(l == depth - 1))
    return xf.reshape(bsz, seq, d)
```
